```python
import jax, jax.numpy as jnp
from jax import lax
import numpy as np

D_MODEL = 1024
BATCH = 4
SEQ = 4096
DEPTH = 4

CHUNK = 64
MEM_LEN = 256
D_MIX = D_MODEL
D_POOL = D_MIX // 4
D_SCONV = (D_MIX - D_POOL) // 2
D_CONF = D_MIX - D_POOL - D_SCONV
POOL_WINDOWS = (2, 4, 8, 16)
N_POOL_GROUPS = len(POOL_WINDOWS)
POOL_GROUP = D_POOL // N_POOL_GROUPS
SCONV_WIDTH = 3
CONF_WIDTH = 31
SPLITS = (D_POOL, D_POOL + D_SCONV, D_POOL + 2 * D_SCONV, D_POOL + 3 * D_SCONV,
          D_POOL + 3 * D_SCONV + D_CONF)
D_IN = D_POOL + 3 * D_SCONV + 2 * D_CONF
N_MEM_HEADS = 4
MEM_HEAD_DIM = D_MODEL // N_MEM_HEADS
D_FF = 2816
N_EXPERTS = 8
TOP_K = 2
D_FF_EXPERT = 3584
N_DENSE = (DEPTH + 1) // 2
N_MOE = DEPTH // 2
EPS = 1e-6

kernel_name = "hybrid_pool_sconv_conformer_moe_trunk"


def rms_norm(x, g):
    xf = x.astype(jnp.float32)
    y = xf * lax.rsqrt(jnp.mean(xf * xf, axis=-1, keepdims=True) + EPS)
    return (y * g.astype(jnp.float32)).astype(x.dtype)


def layer_norm(x, g, b):
    xf = x.astype(jnp.float32)
    mu = jnp.mean(xf, axis=-1, keepdims=True)
    xc = xf - mu
    y = xc * lax.rsqrt(jnp.mean(xc * xc, axis=-1, keepdims=True) + EPS)
    return (y * g.astype(jnp.float32) + b.astype(jnp.float32)).astype(x.dtype)


def causal_depthwise_conv(x, w):
    k, c = w.shape
    return lax.conv_general_dilated(
        x, w[:, None, :].astype(x.dtype), window_strides=(1,), padding=[(k - 1, 0)],
        dimension_numbers=('NWC', 'WIO', 'NWC'), feature_group_count=c)


def multiscale_pool(u, pool_w, pool_scale):
    b, s, _ = u.shape
    uf = u.astype(jnp.float32).reshape(b, s, N_POOL_GROUPS, POOL_GROUP)
    csum = jnp.cumsum(uf, axis=1)
    t = jnp.arange(s, dtype=jnp.int32)
    means = []
    for g, win in enumerate(POOL_WINDOWS):
        cg = csum[:, :, g]
        lagged = jnp.pad(cg, ((0, 0), (win, 0), (0, 0)))[:, :s]
        count = jnp.minimum(t + 1, win).astype(jnp.float32)[None, :, None]
        means.append((cg - lagged) / count)
    pooled = jnp.stack(means, axis=2) - uf
    y = jnp.einsum('bsgc,gcd->bsgd', pooled, pool_w.astype(jnp.float32)).reshape(b, s, D_POOL)
    return (y * pool_scale.astype(jnp.float32)).astype(u.dtype)


def mixer_block(h, w_in, pool_w, pool_scale, sconv_w, conf_w, conf_b, conf_ln_g, conf_ln_b,
                group_norm_g, w_out):
    z = h @ w_in
    u_pool, u_s, gate_b, gate_c, conf_a, conf_gate = jnp.split(z, SPLITS, axis=-1)
    y_a = multiscale_pool(u_pool, pool_w, pool_scale)
    y_b = gate_b * causal_depthwise_conv(gate_c * u_s, sconv_w)
    hc = conf_a * jax.nn.sigmoid(conf_gate)
    hc = causal_depthwise_conv(hc, conf_w) + conf_b
    y_c = jax.nn.silu(layer_norm(hc, conf_ln_g, conf_ln_b))
    y = jnp.concatenate([
        rms_norm(y_a, group_norm_g[:D_POOL]),
        rms_norm(y_b, group_norm_g[D_POOL:D_POOL + D_SCONV]),
        rms_norm(y_c, group_norm_g[D_POOL + D_SCONV:]),
    ], axis=-1)
    return y @ w_out


def memory_cross_attention(h, mem_k, mem_v, w_q, w_o):
    b, s, _ = h.shape
    q = (h @ w_q).reshape(b, s, N_MEM_HEADS, MEM_HEAD_DIM)
    scores = jnp.einsum('bshd,bmhd->bhsm', q, mem_k).astype(jnp.float32) * (MEM_HEAD_DIM ** -0.5)
    probs = jax.nn.softmax(scores, axis=-1).astype(h.dtype)
    o = jnp.einsum('bhsm,bmhd->bshd', probs, mem_v).reshape(b, s, D_MODEL)
    return o @ w_o


def swiglu(h, w_gu, w_down):
    g, u = jnp.split(h @ w_gu, 2, axis=-1)
    return (jax.nn.silu(g) * u) @ w_down


def moe_swiglu(h, w_router, w_gu, w_down):
    b, s, d = h.shape
    t = h.reshape(b * s, d)
    logits = (t @ w_router).astype(jnp.float32)
    top_vals, top_idx = lax.top_k(logits, TOP_K)
    top_w = jax.nn.softmax(top_vals, axis=-1)
    gates = jnp.sum(jax.nn.one_hot(top_idx, N_EXPERTS, dtype=jnp.float32) * top_w[..., None], axis=1)
    gates = gates.astype(t.dtype)
    out = jnp.zeros_like(t)
    for e in range(N_EXPERTS):
        out = out + gates[:, e:e + 1] * swiglu(t, w_gu[e], w_down[e])
    return out.reshape(b, s, d)


def setup_inputs(seed: int = 0) -> dict:
    key = jax.random.key(seed)
    ks = jax.random.split(key, 32)

    def nrm(k, shape, scale):
        return jax.random.normal(k, shape, dtype=jnp.float32) * scale

    def gain(k, shape):
        return 1.0 + 0.1 * jax.random.normal(k, shape, dtype=jnp.float32)

    L = DEPTH
    return {
        'x': nrm(ks[0], (BATCH, SEQ, D_MODEL), 1.0),
        'mem': nrm(ks[1], (BATCH, MEM_LEN, D_MODEL), 1.0),
        'mix_norm_g': gain(ks[2], (L, D_MODEL)),
        'w_in': nrm(ks[3], (L, D_MODEL, D_IN), D_MODEL ** -0.5),
        'pool_w': nrm(ks[4], (L, N_POOL_GROUPS, POOL_GROUP, POOL_GROUP), POOL_GROUP ** -0.5),
        'pool_scale': gain(ks[5], (L, D_POOL)),
        'sconv_w': nrm(ks[6], (L, SCONV_WIDTH, D_SCONV), SCONV_WIDTH ** -0.5),
        'conf_w': nrm(ks[7], (L, CONF_WIDTH, D_CONF), CONF_WIDTH ** -0.5),
        'conf_b': nrm(ks[8], (L, D_CONF), 0.02),
        'conf_ln_g': gain(ks[9], (L, D_CONF)),
        'conf_ln_b': nrm(ks[10], (L, D_CONF), 0.02),
        'group_norm_g': gain(ks[11], (L, D_MIX)),
        'w_out': nrm(ks[12], (L, D_MIX, D_MODEL), D_MIX ** -0.5),
        'mem_norm_g': gain(ks[13], (D_MODEL,)),
        'w_mem_kv': nrm(ks[14], (D_MODEL, 2 * D_MODEL), D_MODEL ** -0.5),
        'xattn_norm_g': gain(ks[15], (L, D_MODEL)),
        'w_q': nrm(ks[16], (L, D_MODEL, D_MODEL), D_MODEL ** -0.5),
        'w_o': nrm(ks[17], (L, D_MODEL, D_MODEL), D_MODEL ** -0.5),
        'ffn_norm_g': gain(ks[18], (L, D_MODEL)),
        'w_gu_dense': nrm(ks[19], (N_DENSE, D_MODEL, 2 * D_FF), D_MODEL ** -0.5),
        'w_down_dense': nrm(ks[20], (N_DENSE, D_FF, D_MODEL), D_FF ** -0.5),
        'w_router': nrm(ks[21], (N_MOE, D_MODEL, N_EXPERTS), D_MODEL ** -0.5),
        'w_gu_moe': nrm(ks[22], (N_MOE, N_EXPERTS, D_MODEL, 2 * D_FF_EXPERT), D_MODEL ** -0.5),
        'w_down_moe': nrm(ks[23], (N_MOE, N_EXPERTS, D_FF_EXPERT, D_MODEL), D_FF_EXPERT ** -0.5),
        'final_norm_g': gain(ks[24], (D_MODEL,)),
    }


def reference(x, mem, mix_norm_g, w_in, pool_w, pool_scale, sconv_w, conf_w, conf_b, conf_ln_g,
              conf_ln_b, group_norm_g, w_out, mem_norm_g, w_mem_kv, xattn_norm_g, w_q, w_o,
              ffn_norm_g, w_gu_dense, w_down_dense, w_router, w_gu_moe, w_down_moe, final_norm_g):
    b, m, _ = mem.shape
    mem_k, mem_v = jnp.split(rms_norm(mem, mem_norm_g) @ w_mem_kv, 2, axis=-1)
    mem_k = mem_k.reshape(b, m, N_MEM_HEADS, MEM_HEAD_DIM)
    mem_v = mem_v.reshape(b, m, N_MEM_HEADS, MEM_HEAD_DIM)
    for l in range(DEPTH):
        x = x + mixer_block(rms_norm(x, mix_norm_g[l]), w_in[l], pool_w[l], pool_scale[l], sconv_w[l],
                            conf_w[l], conf_b[l], conf_ln_g[l], conf_ln_b[l], group_norm_g[l], w_out[l])
        x = x + memory_cross_attention(rms_norm(x, xattn_norm_g[l]), mem_k, mem_v, w_q[l], w_o[l])
        h = rms_norm(x, ffn_norm_g[l])
        if l % 2 == 0:
            x = x + swiglu(h, w_gu_dense[l // 2], w_down_dense[l // 2])
        else:
            x = x + moe_swiglu(h, w_router[l // 2], w_gu_moe[l // 2], w_down_moe[l // 2])
    return rms_norm(x, final_norm_g)
```

```python
import functools

import jax
import jax.numpy as jnp
from jax import lax
from jax.experimental import pallas as pl
from jax.experimental.pallas import tpu as pltpu

F32 = jnp.float32
BF16 = jnp.bfloat16
EPS = 1e-6

LANES = 128
SUBLANES = 8
VMEM_LIMIT_BYTES = 56 * 1024 * 1024

POOL_WINDOWS = (2, 4, 8, 16)
N_MEM_HEADS = 4
N_EXPERTS = 8
HALO = 32

TS_MIX = 512
RC_MIX = 64
TS_ATT = 512
TS_FFN = 512
TM_MOE = 512
FC_MOE = 512
TD_MOE = 512
ZR_MOE = 256


def _params(n_axes=1):
    return pltpu.CompilerParams(dimension_semantics=("arbitrary",) * n_axes,
                                vmem_limit_bytes=VMEM_LIMIT_BYTES)


def _const_spec(shape):
    nd = len(shape)
    return pl.BlockSpec(shape, lambda *_: (0,) * nd, pipeline_mode=pl.Buffered(1))


def _rms(x, g):
    ms = jnp.mean(x * x, axis=-1, keepdims=True)
    return x * lax.rsqrt(ms + EPS) * g


def _dot(a, b):
    return jnp.dot(a, b, preferred_element_type=F32)


def _sigmoid(x):
    return 1.0 / (1.0 + jnp.exp(-x))


def _memkv_kernel(mem_ref, g_ref, w_ref, kv_ref):
    h = _rms(mem_ref[...], g_ref[...]).astype(BF16)
    kv_ref[...] = _dot(h, w_ref[...]).astype(BF16)


def _memkv(mem2d, g, w_kv, m_len):
    n, d = mem2d.shape
    return pl.pallas_call(
        _memkv_kernel,
        grid=(n // m_len,),
        in_specs=[pl.BlockSpec((m_len, d), lambda i: (i, 0)),
                  _const_spec((1, d)),
                  _const_spec(w_kv.shape)],
        out_specs=pl.BlockSpec((m_len, w_kv.shape[1]), lambda i: (i, 0)),
        out_shape=jax.ShapeDtypeStruct((n, w_kv.shape[1]), BF16),
        compiler_params=_params(),
        name="mem_kv",
    )(mem2d, g, w_kv)


def _mixer_kernel(tiles_per_seq, ts, rc, d_pool, d_sc,
                  x_ref, g_ref, win_ref, wbd_ref, pscale_ref, sconv_ref, confw_ref, confb_ref,
                  lng_ref, lnb_ref, gng_ref, wout_ref, o_ref, z_ref, ext_ref, y_ref):
    c0, c1, c2, c3, c4 = d_pool, d_pool + d_sc, d_pool + 2 * d_sc, d_pool + 3 * d_sc, d_pool + 4 * d_sc
    c5 = d_pool + 5 * d_sc
    d_mix = d_pool + 2 * d_sc
    n_taps = confw_ref.shape[0]
    grp = d_pool // len(POOL_WINDOWS)
    i = pl.program_id(0)
    tile_in_seq = i % tiles_per_seq

    @pl.when(tile_in_seq == 0)
    def _():
        ext_ref[0:HALO, :] = jnp.zeros((HALO, d_mix), F32)

    h = _rms(x_ref[...], g_ref[...]).astype(BF16)
    z_ref[...] = _dot(h, win_ref[...])

    lane = lax.broadcasted_iota(jnp.int32, (rc, d_pool), 1)
    row = lax.broadcasted_iota(jnp.int32, (rc, d_pool), 0)
    win = jnp.where(lane < grp, POOL_WINDOWS[0],
                    jnp.where(lane < 2 * grp, POOL_WINDOWS[1],
                              jnp.where(lane < 3 * grp, POOL_WINDOWS[2], POOL_WINDOWS[3])))

    def chunk(c, carry):
        r0 = pl.multiple_of(c * rc, rc)

        def zc(lo, hi):
            return z_ref[pl.ds(r0, rc), lo:hi]

        ext_ref[pl.ds(HALO + r0, rc), 0:c0] = zc(0, c0)
        ext_ref[pl.ds(HALO + r0, rc), c0:c1] = zc(c2, c3) * zc(c0, c1)
        ext_ref[pl.ds(HALO + r0, rc), c1:d_mix] = zc(c3, c4) * _sigmoid(zc(c4, c5))

        e = ext_ref[pl.ds(r0 + 8, rc + 24), 0:c0]
        s2 = e + pltpu.roll(e, 1, 0)
        s4 = s2 + pltpu.roll(s2, 2, 0)
        s8 = s4 + pltpu.roll(s4, 4, 0)
        s16 = s8[24:] + s8[16:16 + rc]
        ssum = jnp.where(lane < grp, s2[24:],
                         jnp.where(lane < 2 * grp, s4[24:],
                                   jnp.where(lane < 3 * grp, s8[24:], s16)))
        tpos = tile_in_seq * ts + r0 + row
        cnt = jnp.minimum(tpos + 1, win).astype(F32)
        pooled = ssum / cnt - e[24:]
        ya = _dot(pooled.astype(BF16), wbd_ref[...]) * pscale_ref[...]
        y_ref[pl.ds(r0, rc), 0:c0] = _rms(ya, gng_ref[:, 0:c0]).astype(BF16)

        sx = ext_ref[pl.ds(r0 + 24, rc + 8), c0:c1]
        conv = (sx[8:] * sconv_ref[2:3, :]
                + pltpu.roll(sx, 1, 0)[8:] * sconv_ref[1:2, :]
                + pltpu.roll(sx, 2, 0)[8:] * sconv_ref[0:1, :])
        yb = zc(c1, c2) * conv
        y_ref[pl.ds(r0, rc), c0:c1] = _rms(yb, gng_ref[:, c0:c1]).astype(BF16)

        acc = None
        for b in range(SUBLANES):
            pb = None
            for a in range((n_taps + SUBLANES - 1) // SUBLANES):
                d = SUBLANES * a + b
                if d >= n_taps:
                    continue
                xa = ext_ref[pl.ds(r0 + 24 - SUBLANES * a, rc + 8), c1:d_mix]
                k = n_taps - 1 - d
                term = xa * confw_ref[k:k + 1, :]
                pb = term if pb is None else pb + term
            if b:
                pb = pltpu.roll(pb, b, 0)
            acc = pb[8:] if acc is None else acc + pb[8:]
        hc = acc + confb_ref[...]
        mu = jnp.mean(hc, axis=-1, keepdims=True)
        xc = hc - mu
        var = jnp.mean(xc * xc, axis=-1, keepdims=True)
        ln = xc * lax.rsqrt(var + EPS) * lng_ref[...] + lnb_ref[...]
        yc = ln * _sigmoid(ln)
        y_ref[pl.ds(r0, rc), c1:d_mix] = _rms(yc, gng_ref[:, c1:d_mix]).astype(BF16)
        return carry

    lax.fori_loop(0, ts // rc, chunk, 0)

    o_ref[...] = x_ref[...] + _dot(y_ref[...], wout_ref[...])
    ext_ref[0:HALO, :] = ext_ref[ts:ts + HALO, :]


def _mixer(x, seq_len, g, w_in, w_bd, pscale, sconv_w, conf_w, conf_b, ln_g, ln_b, gn_g, w_out):
    n, d = x.shape
    ts = min(TS_MIX, seq_len)
    rc = min(RC_MIX, ts)
    d_pool = w_bd.shape[0]
    d_sc = sconv_w.shape[1]
    d_mix = w_out.shape[0]
    assert conf_w.shape[0] <= HALO - 1 and seq_len % ts == 0 and ts % rc == 0
    kern = functools.partial(_mixer_kernel, seq_len // ts, ts, rc, d_pool, d_sc)
    consts = (g, w_in, w_bd, pscale, sconv_w, conf_w, conf_b, ln_g, ln_b, gn_g, w_out)
    return pl.pallas_call(
        kern,
        grid=(n // ts,),
        in_specs=[pl.BlockSpec((ts, d), lambda i: (i, 0))] + [_const_spec(c.shape) for c in consts],
        out_specs=pl.BlockSpec((ts, d), lambda i: (i, 0)),
        out_shape=jax.ShapeDtypeStruct((n, d), F32),
        scratch_shapes=[pltpu.VMEM((ts, w_in.shape[1]), F32),
                        pltpu.VMEM((HALO + ts, d_mix), F32),
                        pltpu.VMEM((ts, d_mix), BF16)],
        compiler_params=_params(),
        name="mixer",
    )(x, *consts)


def _attn_body(x_ref, g_ref, wq_ref, k_ref, v_ref, wo_ref):
    d = x_ref.shape[1]
    dh = d // N_MEM_HEADS
    x = x_ref[...]
    h = _rms(x, g_ref[...]).astype(BF16)
    q = _dot(h, wq_ref[...])
    outs = []
    for hh in range(N_MEM_HEADS):
        sl = slice(hh * dh, (hh + 1) * dh)
        s = lax.dot_general(q[:, sl].astype(BF16), k_ref[:, sl], (((1,), (1,)), ((), ())),
                            preferred_element_type=F32) * (dh ** -0.5)
        e = jnp.exp(s - jnp.max(s, axis=-1, keepdims=True))
        p = e * (1.0 / jnp.sum(e, axis=-1, keepdims=True))
        outs.append(_dot(p.astype(BF16), v_ref[:, sl]).astype(BF16))
    return x + _dot(jnp.concatenate(outs, axis=-1), wo_ref[...])


def _attn_kernel(x_ref, g_ref, wq_ref, k_ref, v_ref, wo_ref, o_ref):
    o_ref[...] = _attn_body(x_ref, g_ref, wq_ref, k_ref, v_ref, wo_ref)


def _attn_route_kernel(x_ref, g_ref, wq_ref, k_ref, v_ref, wo_ref, fg_ref, wr_ref, tri_ref,
                       o_ref, h2_ref, ri_ref, rg_ref, cnt_ref, carry_ref):
    i = pl.program_id(0)

    @pl.when(i == 0)
    def _():
        carry_ref[...] = jnp.zeros(carry_ref.shape, F32)

    xn = _attn_body(x_ref, g_ref, wq_ref, k_ref, v_ref, wo_ref)
    o_ref[...] = xn
    h2 = _rms(xn, fg_ref[...])
    h2_ref[...] = h2

    h_hi = h2.astype(BF16)
    h_lo = (h2 - h_hi.astype(F32)).astype(BF16)
    part = _dot(h_hi, wr_ref[...])
    logits = part[:, 0:LANES] + part[:, LANES:2 * LANES] + _dot(h_lo, wr_ref[:, 0:LANES])

    ts = x_ref.shape[0]
    lane = lax.broadcasted_iota(jnp.int32, (ts, LANES), 1).astype(F32)
    neg = jnp.float32(-jnp.inf)
    lg = jnp.where(lane < N_EXPERTS, logits, neg)
    v1 = jnp.max(lg, axis=-1, keepdims=True)
    i1 = jnp.min(jnp.where(lg == v1, lane, float(LANES)), axis=-1, keepdims=True)
    lg2 = jnp.where(lane == i1, neg, lg)
    v2 = jnp.max(lg2, axis=-1, keepdims=True)
    i2 = jnp.min(jnp.where(lg2 == v2, lane, float(LANES)), axis=-1, keepdims=True)
    e2 = jnp.exp(v2 - v1)
    den = 1.0 + e2
    w1 = 1.0 / den
    w2 = e2 / den

    sel1 = lane == i1
    sel2 = lane == i2
    onehot = jnp.where(sel1 | sel2, 1.0, 0.0)
    ranks = carry_ref[0:1, :] + _dot(tri_ref[...], onehot.astype(BF16))
    r1 = jnp.sum(jnp.where(sel1, ranks, 0.0), axis=-1, keepdims=True)
    r2 = jnp.sum(jnp.where(sel2, ranks, 0.0), axis=-1, keepdims=True)
    total = carry_ref[0:1, :] + jnp.sum(onehot, axis=0, keepdims=True)
    carry_ref[...] = jnp.broadcast_to(total, carry_ref.shape)
    cnt_ref[...] = jnp.broadcast_to(total, cnt_ref.shape)

    meta = jnp.where(lane == 0, i1, jnp.where(lane == 1, i2, jnp.where(lane == 2, r1, jnp.where(lane == 3, r2, 0.0))))
    ri_ref[...] = meta.astype(jnp.int32)
    rg_ref[...] = jnp.where(lane == 0, w1, jnp.where(lane == 1, w2, 0.0))


def _attn(x, seq_len, m_len, g, w_q, kv, w_o, route=None):
    n, d = x.shape
    ts = min(TS_ATT, seq_len)
    tps = seq_len // ts
    x_spec = pl.BlockSpec((ts, d), lambda i: (i, 0))
    in_specs = [x_spec, _const_spec((1, d)), _const_spec(w_q.shape),
                pl.BlockSpec((m_len, d), lambda i: (i // tps, 0)),
                pl.BlockSpec((m_len, d), lambda i: (i // tps, 1)),
                _const_spec(w_o.shape)]
    args = [x, g, w_q, kv, kv, w_o]
    if route is None:
        return pl.pallas_call(
            _attn_kernel, grid=(n // ts,), in_specs=in_specs, out_specs=x_spec,
            out_shape=jax.ShapeDtypeStruct((n, d), F32), compiler_params=_params(), name="attn",
        )(*args)
    ffn_g, w_r2 = route
    tri = jnp.tril(jnp.ones((ts, ts), BF16), -1)
    lane_spec = pl.BlockSpec((ts, LANES), lambda i: (i, 0))
    return pl.pallas_call(
        _attn_route_kernel, grid=(n // ts,),
        in_specs=in_specs + [_const_spec((1, d)), _const_spec(w_r2.shape), _const_spec(tri.shape)],
        out_specs=[x_spec, x_spec, lane_spec, lane_spec,
                   pl.BlockSpec((SUBLANES, LANES), lambda i: (0, 0))],
        out_shape=[jax.ShapeDtypeStruct((n, d), F32), jax.ShapeDtypeStruct((n, d), F32),
                   jax.ShapeDtypeStruct((n, LANES), jnp.int32), jax.ShapeDtypeStruct((n, LANES), F32),
                   jax.ShapeDtypeStruct((SUBLANES, LANES), F32)],
        scratch_shapes=[pltpu.VMEM((SUBLANES, LANES), F32)],
        compiler_params=_params(), name="attn_route",
    )(*args, ffn_g, w_r2, tri)


def _ffn_kernel(chunks, d_ff, x_ref, g_ref, wgu_ref, wd_ref, o_ref):
    x = x_ref[...]
    h = _rms(x, g_ref[...]).astype(BF16)
    acc = x
    for lo, hi in chunks:
        gate = _dot(h, wgu_ref[:, lo:hi])
        up = _dot(h, wgu_ref[:, d_ff + lo:d_ff + hi])
        a = (gate * _sigmoid(gate) * up).astype(BF16)
        acc = acc + _dot(a, wd_ref[lo:hi, :])
    o_ref[...] = acc


def _ffn(x, g, w_gu, w_down):
    n, d = x.shape
    ts = min(TS_FFN, n)
    d_ff = w_down.shape[0]
    step = 1024 if d_ff > 1024 else d_ff
    chunks = tuple((lo, min(lo + step, d_ff)) for lo in range(0, d_ff, step))
    x_spec = pl.BlockSpec((ts, d), lambda i: (i, 0))
    return pl.pallas_call(
        functools.partial(_ffn_kernel, chunks, d_ff), grid=(n // ts,),
        in_specs=[x_spec, _const_spec((1, d)), _const_spec(w_gu.shape), _const_spec(w_down.shape)],
        out_specs=x_spec, out_shape=jax.ShapeDtypeStruct((n, d), F32),
        compiler_params=_params(), name="ffn_dense",
    )(x, g, w_gu, w_down)


def _row_copy(src, src_row, dst, dst_row, sem):
    return pltpu.make_async_copy(src.at[pl.ds(src_row, 1)], dst.at[pl.ds(dst_row, 1)], sem)


def _dispatch_kernel(td, tm, starts_ref, ntiles_ref, meta_ref, h_hbm, xs_hbm, zero_ref, zsem, sem):
    i = pl.program_id(0)
    zr = zero_ref.shape[0]

    @pl.when(i == 0)
    def _():
        zero_ref[...] = jnp.zeros(zero_ref.shape, F32)

        def zero_copies(e):
            base = pl.multiple_of(starts_ref[e] + (ntiles_ref[e] - 1) * tm, tm)
            return [pltpu.make_async_copy(zero_ref, xs_hbm.at[pl.ds(base + q * zr, zr)], zsem)
                    for q in range(tm // zr)]

        for e in range(N_EXPERTS):
            @pl.when(ntiles_ref[e] > 0)
            def _():
                for cp in zero_copies(e):
                    cp.start()
        for e in range(N_EXPERTS):
            @pl.when(ntiles_ref[e] > 0)
            def _():
                for cp in zero_copies(e):
                    cp.wait()

        n_used = ntiles_ref[0]
        for e in range(1, N_EXPERTS):
            n_used = n_used + ntiles_ref[e]

        def clear_tile(j, carry):
            base = pl.multiple_of(j * tm, tm)
            copies = [pltpu.make_async_copy(zero_ref, xs_hbm.at[pl.ds(base + q * zr, zr)], zsem)
                      for q in range(tm // zr)]
            for cp in copies:
                cp.start()
            for cp in copies:
                cp.wait()
            return carry

        lax.fori_loop(n_used, xs_hbm.shape[0] // tm, clear_tile, 0)

    def issue(t, carry):
        for k in range(2):
            pos = starts_ref[meta_ref[0, 0, 4 * t + k]] + meta_ref[0, 0, 4 * t + 2 + k]
            _row_copy(h_hbm, i * td + t, xs_hbm, pos, sem).start()
        return carry

    def drain(t, carry):
        for k in range(2):
            _row_copy(h_hbm, 0, xs_hbm, 0, sem).wait()
        return carry

    lax.fori_loop(0, td, issue, 0)
    lax.fori_loop(0, td, drain, 0)


def _dispatch(h2, meta, starts, ntiles, r_pad, td, tm):
    n, d = h2.shape
    zr = min(ZR_MOE, tm)
    grid_spec = pltpu.PrefetchScalarGridSpec(
        num_scalar_prefetch=2, grid=(n // td,),
        in_specs=[pl.BlockSpec((1, 1, 4 * td), lambda i, *_: (i, 0, 0), memory_space=pltpu.SMEM),
                  pl.BlockSpec(memory_space=pl.ANY)],
        out_specs=pl.BlockSpec(memory_space=pl.ANY),
        scratch_shapes=[pltpu.VMEM((zr, d), F32), pltpu.SemaphoreType.DMA, pltpu.SemaphoreType.DMA])
    return pl.pallas_call(
        functools.partial(_dispatch_kernel, td, tm), grid_spec=grid_spec,
        out_shape=jax.ShapeDtypeStruct((r_pad, d), F32),
        compiler_params=_params(), name="moe_dispatch",
    )(starts, ntiles, meta, h2)


def _gmm_kernel(te_ref, nu_ref, xs_ref, wg_ref, wu_ref, wd_ref, o_ref):
    i = pl.program_id(0)
    c = pl.program_id(1)

    @pl.when(i < nu_ref[0])
    def _():
        x = xs_ref[...].astype(BF16)
        gate = _dot(x, wg_ref[...])
        up = _dot(x, wu_ref[...])
        a = (gate * _sigmoid(gate) * up).astype(BF16)
        y = _dot(a, wd_ref[...])

        @pl.when(c == 0)
        def _():
            o_ref[...] = y

        @pl.when(c > 0)
        def _():
            o_ref[...] += y

    @pl.when(jnp.logical_and(i >= nu_ref[0], c == 0))
    def _():
        o_ref[...] = jnp.zeros(o_ref.shape, F32)


def _gmm(xs, w_gu, w_down, tile_expert, n_used, tm, fc):
    r_pad, d = xs.shape
    d_ff = w_down.shape[1]
    n_chunks = d_ff // fc
    n_tiles = r_pad // tm

    def tile(i, nu):
        return jnp.minimum(i, nu[0] - 1)

    def chunk(i, c, nu):
        return jnp.where(i < nu[0], c, n_chunks - 1)

    grid_spec = pltpu.PrefetchScalarGridSpec(
        num_scalar_prefetch=2, grid=(n_tiles, n_chunks),
        in_specs=[pl.BlockSpec((tm, d), lambda i, c, te, nu: (tile(i, nu), 0)),
                  pl.BlockSpec((None, d, fc), lambda i, c, te, nu: (te[tile(i, nu)], 0, chunk(i, c, nu))),
                  pl.BlockSpec((None, d, fc),
                               lambda i, c, te, nu: (te[tile(i, nu)], 0, n_chunks + chunk(i, c, nu))),
                  pl.BlockSpec((None, fc, d), lambda i, c, te, nu: (te[tile(i, nu)], chunk(i, c, nu), 0))],
        out_specs=pl.BlockSpec((tm, d), lambda i, c, te, nu: (i, 0)))
    return pl.pallas_call(
        _gmm_kernel, grid_spec=grid_spec, out_shape=jax.ShapeDtypeStruct((r_pad, d), F32),
        compiler_params=_params(2), name="moe_gmm",
    )(tile_expert, n_used, xs, w_gu, w_gu, w_down)


def _combine_kernel(td, final, starts_ref, meta_ref, x_ref, rg_ref, fg_ref, ys_hbm, o_ref, ybuf, sem):
    def issue(t, carry):
        for k in range(2):
            pos = starts_ref[meta_ref[0, 0, 4 * t + k]] + meta_ref[0, 0, 4 * t + 2 + k]
            _row_copy(ys_hbm, pos, ybuf.at[k], t, sem).start()
        return carry

    def drain(t, carry):
        for k in range(2):
            _row_copy(ys_hbm, 0, ybuf.at[k], 0, sem).wait()
        return carry

    lax.fori_loop(0, td, issue, 0)
    lax.fori_loop(0, td, drain, 0)
    out = x_ref[...] + rg_ref[:, 0:1] * ybuf[0] + rg_ref[:, 1:2] * ybuf[1]
    if final:
        out = _rms(out, fg_ref[...])
    o_ref[...] = out


def _combine(x, rg, meta, starts, ys, final_g, final, td):
    n, d = x.shape
    x_spec = pl.BlockSpec((td, d), lambda i, *_: (i, 0))
    grid_spec = pltpu.PrefetchScalarGridSpec(
        num_scalar_prefetch=1, grid=(n // td,),
        in_specs=[pl.BlockSpec((1, 1, 4 * td), lambda i, *_: (i, 0, 0), memory_space=pltpu.SMEM),
                  x_spec,
                  pl.BlockSpec((td, LANES), lambda i, *_: (i, 0)),
                  pl.BlockSpec((1, d), lambda i, *_: (0, 0)),
                  pl.BlockSpec(memory_space=pl.ANY)],
        out_specs=x_spec,
        scratch_shapes=[pltpu.VMEM((2, td, d), F32), pltpu.SemaphoreType.DMA])
    return pl.pallas_call(
        functools.partial(_combine_kernel, td, final), grid_spec=grid_spec,
        out_shape=jax.ShapeDtypeStruct((n, d), F32),
        compiler_params=_params(), name="moe_combine",
    )(starts, meta, x, rg, final_g, ys)


def _moe(x, h2, ri, rg, counts, w_gu, w_down, final_g, final):
    n, d = x.shape
    tm = min(TM_MOE, n)
    td = min(TD_MOE, n)
    fc = min(FC_MOE, w_down.shape[1])
    max_tiles = (2 * n) // tm + N_EXPERTS
    cnt = counts[0, :N_EXPERTS].astype(jnp.int32)
    ntiles = (cnt + tm - 1) // tm
    cum = jnp.cumsum(ntiles)
    starts = (cum - ntiles) * tm
    n_used = cum[-1:]
    tile_expert = jnp.minimum(
        jnp.searchsorted(cum, jnp.arange(max_tiles, dtype=jnp.int32), side="right"),
        N_EXPERTS - 1).astype(jnp.int32)
    meta = ri[:, 0:4].reshape(n // td, 1, 4 * td)
    xs = _dispatch(h2, meta, starts, ntiles, max_tiles * tm, td, tm)
    ys = _gmm(xs, w_gu, w_down, tile_expert, n_used, tm, fc)
    return _combine(x, rg, meta, starts, ys, final_g, final, td)


def _row(v):
    return v.reshape(1, -1).astype(F32)


def _block_diag(pool_w):
    g, c, _ = pool_w.shape
    out = jnp.zeros((g * c, g * c), pool_w.dtype)
    for j in range(g):
        out = lax.dynamic_update_slice(out, pool_w[j], (j * c, j * c))
    return out


def _router_hi_lo(w_router):
    d, e = w_router.shape
    hi = w_router.astype(BF16)
    lo = (w_router - hi.astype(F32)).astype(BF16)
    pad = jnp.zeros((d, LANES - e), BF16)
    return jnp.concatenate([hi, pad, lo, pad], axis=1)


def kernel(x, mem, mix_norm_g, w_in, pool_w, pool_scale, sconv_w, conf_w, conf_b, conf_ln_g, conf_ln_b, group_norm_g, w_out, mem_norm_g, w_mem_kv, xattn_norm_g, w_q, w_o, ffn_norm_g, w_gu_dense, w_down_dense, w_router, w_gu_moe, w_down_moe, final_norm_g):
    b, s, d = x.shape
    m_len = mem.shape[1]
    depth = w_in.shape[0]
    assert depth % 2 == 0, "the last layer must be a routed layer (it applies the final norm)"
    kv = _memkv(mem.reshape(b * m_len, d), _row(mem_norm_g), w_mem_kv.astype(BF16), m_len)
    xf = x.reshape(b * s, d)
    for l in range(depth):
        xf = _mixer(xf, s, _row(mix_norm_g[l]), w_in[l].astype(BF16), _block_diag(pool_w[l]).astype(BF16),
                    _row(pool_scale[l]), sconv_w[l], conf_w[l], _row(conf_b[l]), _row(conf_ln_g[l]),
                    _row(conf_ln_b[l]), _row(group_norm_g[l]), w_out[l].astype(BF16))
        if l % 2 == 0:
            xf = _attn(xf, s, m_len, _row(xattn_norm_g[l]), w_q[l].astype(BF16), kv, w_o[l].astype(BF16))
            xf = _ffn(xf, _row(ffn_norm_g[l]), w_gu_dense[l // 2].astype(BF16), w_down_dense[l // 2].astype(BF16))
        else:
            xf, h2, ri, rg, counts = _attn(
                xf, s, m_len, _row(xattn_norm_g[l]), w_q[l].astype(BF16), kv, w_o[l].astype(BF16),
                route=(_row(ffn_norm_g[l]), _router_hi_lo(w_router[l // 2])))
            xf = _moe(xf, h2, ri, rg, counts, w_gu_moe[l // 2].astype(BF16), w_down_moe[l // 2].astype(BF16),
                      _row(final_norm_g), l == depth - 1)
    return xf.reshape(b, s, d)
```

```python
import functools

import jax
import jax.numpy as jnp
from jax import lax
from jax.experimental import pallas as pl
from jax.experimental.pallas import tpu as pltpu

F32 = jnp.float32
BF16 = jnp.bfloat16
EPS = 1e-6

LANES = 128
SUBLANES = 8
VMEM_LIMIT_BYTES = 56 * 1024 * 1024

POOL_WINDOWS = (2, 4, 8, 16)
N_MEM_HEADS = 4
N_EXPERTS = 8
HALO = 32

TS_MIX = 512
RC_MIX = 64
TS_ATT = 512
TS_FFN = 512
TM_MOE = 1024
FC_MOE = 512
TD_MOE = 512
ZR_MOE = 256


def _params(n_axes=1):
    return pltpu.CompilerParams(dimension_semantics=("arbitrary",) * n_axes,
                                vmem_limit_bytes=VMEM_LIMIT_BYTES)


def _const_spec(shape):
    nd = len(shape)
    return pl.BlockSpec(shape, lambda *_: (0,) * nd, pipeline_mode=pl.Buffered(1))


def _rms(x, g):
    ms = jnp.mean(x * x, axis=-1, keepdims=True)
    return x * lax.rsqrt(ms + EPS) * g


def _dot(a, b):
    return jnp.dot(a, b, preferred_element_type=F32)


def _sigmoid(x):
    return 1.0 / (1.0 + jnp.exp(-x))


def _memkv_kernel(mem_ref, g_ref, w_ref, kv_ref):
    h = _rms(mem_ref[...], g_ref[...]).astype(BF16)
    kv_ref[...] = _dot(h, w_ref[...]).astype(BF16)


def _memkv(mem2d, g, w_kv, m_len):
    n, d = mem2d.shape
    return pl.pallas_call(
        _memkv_kernel,
        grid=(n // m_len,),
        in_specs=[pl.BlockSpec((m_len, d), lambda i: (i, 0)),
                  _const_spec((1, d)),
                  _const_spec(w_kv.shape)],
        out_specs=pl.BlockSpec((m_len, w_kv.shape[1]), lambda i: (i, 0)),
        out_shape=jax.ShapeDtypeStruct((n, w_kv.shape[1]), BF16),
        compiler_params=_params(),
        name="mem_kv",
    )(mem2d, g, w_kv)


def _mixer_kernel(tiles_per_seq, ts, rc, d_pool, d_sc,
                  x_ref, g_ref, win_ref, wbd_ref, pscale_ref, sconv_ref, confw_ref, confb_ref,
                  lng_ref, lnb_ref, gng_ref, wout_ref, o_ref, z_ref, ext_ref, y_ref):
    c0, c1, c2, c3, c4 = d_pool, d_pool + d_sc, d_pool + 2 * d_sc, d_pool + 3 * d_sc, d_pool + 4 * d_sc
    c5 = d_pool + 5 * d_sc
    d_mix = d_pool + 2 * d_sc
    n_taps = confw_ref.shape[0]
    grp = d_pool // len(POOL_WINDOWS)
    i = pl.program_id(0)
    tile_in_seq = i % tiles_per_seq

    @pl.when(tile_in_seq == 0)
    def _():
        ext_ref[0:HALO, :] = jnp.zeros((HALO, d_mix), F32)

    h = _rms(x_ref[...], g_ref[...]).astype(BF16)
    z_ref[...] = _dot(h, win_ref[...])

    lane = lax.broadcasted_iota(jnp.int32, (rc, d_pool), 1)
    row = lax.broadcasted_iota(jnp.int32, (rc, d_pool), 0)
    win = jnp.where(lane < grp, POOL_WINDOWS[0],
                    jnp.where(lane < 2 * grp, POOL_WINDOWS[1],
                              jnp.where(lane < 3 * grp, POOL_WINDOWS[2], POOL_WINDOWS[3])))

    def chunk(c, carry):
        r0 = pl.multiple_of(c * rc, rc)

        def zc(lo, hi):
            return z_ref[pl.ds(r0, rc), lo:hi]

        ext_ref[pl.ds(HALO + r0, rc), 0:c0] = zc(0, c0)
        ext_ref[pl.ds(HALO + r0, rc), c0:c1] = zc(c2, c3) * zc(c0, c1)
        ext_ref[pl.ds(HALO + r0, rc), c1:d_mix] = zc(c3, c4) * _sigmoid(zc(c4, c5))

        e = ext_ref[pl.ds(r0 + 8, rc + 24), 0:c0]
        s2 = e + pltpu.roll(e, 1, 0)
        s4 = s2 + pltpu.roll(s2, 2, 0)
        s8 = s4 + pltpu.roll(s4, 4, 0)
        s16 = s8[24:] + s8[16:16 + rc]
        ssum = jnp.where(lane < grp, s2[24:],
                         jnp.where(lane < 2 * grp, s4[24:],
                                   jnp.where(lane < 3 * grp, s8[24:], s16)))
        tpos = tile_in_seq * ts + r0 + row
        cnt = jnp.minimum(tpos + 1, win).astype(F32)
        pooled = ssum / cnt - e[24:]
        ya = _dot(pooled.astype(BF16), wbd_ref[...]) * pscale_ref[...]
        y_ref[pl.ds(r0, rc), 0:c0] = _rms(ya, gng_ref[:, 0:c0]).astype(BF16)

        sx = ext_ref[pl.ds(r0 + 24, rc + 8), c0:c1]
        conv = (sx[8:] * sconv_ref[2:3, :]
                + pltpu.roll(sx, 1, 0)[8:] * sconv_ref[1:2, :]
                + pltpu.roll(sx, 2, 0)[8:] * sconv_ref[0:1, :])
        yb = zc(c1, c2) * conv
        y_ref[pl.ds(r0, rc), c0:c1] = _rms(yb, gng_ref[:, c0:c1]).astype(BF16)

        acc = None
        for b in range(SUBLANES):
            pb = None
            for a in range((n_taps + SUBLANES - 1) // SUBLANES):
                d = SUBLANES * a + b
                if d >= n_taps:
                    continue
                xa = ext_ref[pl.ds(r0 + 24 - SUBLANES * a, rc + 8), c1:d_mix]
                k = n_taps - 1 - d
                term = xa * confw_ref[k:k + 1, :]
                pb = term if pb is None else pb + term
            if b:
                pb = pltpu.roll(pb, b, 0)
            acc = pb[8:] if acc is None else acc + pb[8:]
        hc = acc + confb_ref[...]
        mu = jnp.mean(hc, axis=-1, keepdims=True)
        xc = hc - mu
        var = jnp.mean(xc * xc, axis=-1, keepdims=True)
        ln = xc * lax.rsqrt(var + EPS) * lng_ref[...] + lnb_ref[...]
        yc = ln * _sigmoid(ln)
        y_ref[pl.ds(r0, rc), c1:d_mix] = _rms(yc, gng_ref[:, c1:d_mix]).astype(BF16)
        return carry

    lax.fori_loop(0, ts // rc, chunk, 0)

    o_ref[...] = x_ref[...] + _dot(y_ref[...], wout_ref[...])
    ext_ref[0:HALO, :] = ext_ref[ts:ts + HALO, :]


def _mixer(x, seq_len, g, w_in, w_bd, pscale, sconv_w, conf_w, conf_b, ln_g, ln_b, gn_g, w_out):
    n, d = x.shape
    ts = min(TS_MIX, seq_len)
    rc = min(RC_MIX, ts)
    d_pool = w_bd.shape[0]
    d_sc = sconv_w.shape[1]
    d_mix = w_out.shape[0]
    assert conf_w.shape[0] <= HALO - 1 and seq_len % ts == 0 and ts % rc == 0
    kern = functools.partial(_mixer_kernel, seq_len // ts, ts, rc, d_pool, d_sc)
    consts = (g, w_in, w_bd, pscale, sconv_w, conf_w, conf_b, ln_g, ln_b, gn_g, w_out)
    return pl.pallas_call(
        kern,
        grid=(n // ts,),
        in_specs=[pl.BlockSpec((ts, d), lambda i: (i, 0))] + [_const_spec(c.shape) for c in consts],
        out_specs=pl.BlockSpec((ts, d), lambda i: (i, 0)),
        out_shape=jax.ShapeDtypeStruct((n, d), F32),
        scratch_shapes=[pltpu.VMEM((ts, w_in.shape[1]), F32),
                        pltpu.VMEM((HALO + ts, d_mix), F32),
                        pltpu.VMEM((ts, d_mix), BF16)],
        compiler_params=_params(),
        name="mixer",
    )(x, *consts)


def _attn_body(x_ref, g_ref, wq_ref, k_ref, v_ref, wo_ref):
    d = x_ref.shape[1]
    dh = d // N_MEM_HEADS
    x = x_ref[...]
    h = _rms(x, g_ref[...]).astype(BF16)
    q = _dot(h, wq_ref[...])
    outs = []
    for hh in range(N_MEM_HEADS):
        sl = slice(hh * dh, (hh + 1) * dh)
        s = lax.dot_general(q[:, sl].astype(BF16), k_ref[:, sl], (((1,), (1,)), ((), ())),
                            preferred_element_type=F32) * (dh ** -0.5)
        e = jnp.exp(s - jnp.max(s, axis=-1, keepdims=True))
        p = e * (1.0 / jnp.sum(e, axis=-1, keepdims=True))
        outs.append(_dot(p.astype(BF16), v_ref[:, sl]).astype(BF16))
    return x + _dot(jnp.concatenate(outs, axis=-1), wo_ref[...])


def _attn_kernel(x_ref, g_ref, wq_ref, k_ref, v_ref, wo_ref, o_ref):
    o_ref[...] = _attn_body(x_ref, g_ref, wq_ref, k_ref, v_ref, wo_ref)


def _attn_route_kernel(x_ref, g_ref, wq_ref, k_ref, v_ref, wo_ref, fg_ref, wr_ref, tri_ref,
                       o_ref, h2_ref, ri_ref, rg_ref, cnt_ref, carry_ref):
    i = pl.program_id(0)

    @pl.when(i == 0)
    def _():
        carry_ref[...] = jnp.zeros(carry_ref.shape, F32)

    xn = _attn_body(x_ref, g_ref, wq_ref, k_ref, v_ref, wo_ref)
    o_ref[...] = xn
    h2 = _rms(xn, fg_ref[...])
    h2_ref[...] = h2

    h_hi = h2.astype(BF16)
    h_lo = (h2 - h_hi.astype(F32)).astype(BF16)
    part = _dot(h_hi, wr_ref[...])
    logits = part[:, 0:LANES] + part[:, LANES:2 * LANES] + _dot(h_lo, wr_ref[:, 0:LANES])

    ts = x_ref.shape[0]
    lane = lax.broadcasted_iota(jnp.int32, (ts, LANES), 1).astype(F32)
    neg = jnp.float32(-jnp.inf)
    lg = jnp.where(lane < N_EXPERTS, logits, neg)
    v1 = jnp.max(lg, axis=-1, keepdims=True)
    i1 = jnp.min(jnp.where(lg == v1, lane, float(LANES)), axis=-1, keepdims=True)
    lg2 = jnp.where(lane == i1, neg, lg)
    v2 = jnp.max(lg2, axis=-1, keepdims=True)
    i2 = jnp.min(jnp.where(lg2 == v2, lane, float(LANES)), axis=-1, keepdims=True)
    e2 = jnp.exp(v2 - v1)
    den = 1.0 + e2
    w1 = 1.0 / den
    w2 = e2 / den

    sel1 = lane == i1
    sel2 = lane == i2
    onehot = jnp.where(sel1 | sel2, 1.0, 0.0)
    ranks = carry_ref[0:1, :] + _dot(tri_ref[...], onehot.astype(BF16))
    r1 = jnp.sum(jnp.where(sel1, ranks, 0.0), axis=-1, keepdims=True)
    r2 = jnp.sum(jnp.where(sel2, ranks, 0.0), axis=-1, keepdims=True)
    total = carry_ref[0:1, :] + jnp.sum(onehot, axis=0, keepdims=True)
    carry_ref[...] = jnp.broadcast_to(total, carry_ref.shape)
    cnt_ref[...] = jnp.broadcast_to(total, cnt_ref.shape)

    meta = jnp.where(lane == 0, i1, jnp.where(lane == 1, i2, jnp.where(lane == 2, r1, jnp.where(lane == 3, r2, 0.0))))
    ri_ref[...] = meta.astype(jnp.int32)
    rg_ref[...] = jnp.where(lane == 0, w1, jnp.where(lane == 1, w2, 0.0))


def _attn(x, seq_len, m_len, g, w_q, kv, w_o, route=None):
    n, d = x.shape
    ts = min(TS_ATT, seq_len)
    tps = seq_len // ts
    x_spec = pl.BlockSpec((ts, d), lambda i: (i, 0))
    in_specs = [x_spec, _const_spec((1, d)), _const_spec(w_q.shape),
                pl.BlockSpec((m_len, d), lambda i: (i // tps, 0)),
                pl.BlockSpec((m_len, d), lambda i: (i // tps, 1)),
                _const_spec(w_o.shape)]
    args = [x, g, w_q, kv, kv, w_o]
    if route is None:
        return pl.pallas_call(
            _attn_kernel, grid=(n // ts,), in_specs=in_specs, out_specs=x_spec,
            out_shape=jax.ShapeDtypeStruct((n, d), F32), compiler_params=_params(), name="attn",
        )(*args)
    ffn_g, w_r2 = route
    tri = jnp.tril(jnp.ones((ts, ts), BF16), -1)
    lane_spec = pl.BlockSpec((ts, LANES), lambda i: (i, 0))
    return pl.pallas_call(
        _attn_route_kernel, grid=(n // ts,),
        in_specs=in_specs + [_const_spec((1, d)), _const_spec(w_r2.shape), _const_spec(tri.shape)],
        out_specs=[x_spec, x_spec, lane_spec, lane_spec,
                   pl.BlockSpec((SUBLANES, LANES), lambda i: (0, 0))],
        out_shape=[jax.ShapeDtypeStruct((n, d), F32), jax.ShapeDtypeStruct((n, d), F32),
                   jax.ShapeDtypeStruct((n, LANES), jnp.int32), jax.ShapeDtypeStruct((n, LANES), F32),
                   jax.ShapeDtypeStruct((SUBLANES, LANES), F32)],
        scratch_shapes=[pltpu.VMEM((SUBLANES, LANES), F32)],
        compiler_params=_params(), name="attn_route",
    )(*args, ffn_g, w_r2, tri)


def _ffn_kernel(chunks, d_ff, x_ref, g_ref, wgu_ref, wd_ref, o_ref):
    x = x_ref[...]
    h = _rms(x, g_ref[...]).astype(BF16)
    acc = x
    for lo, hi in chunks:
        gate = _dot(h, wgu_ref[:, lo:hi])
        up = _dot(h, wgu_ref[:, d_ff + lo:d_ff + hi])
        a = (gate * _sigmoid(gate) * up).astype(BF16)
        acc = acc + _dot(a, wd_ref[lo:hi, :])
    o_ref[...] = acc


def _ffn(x, g, w_gu, w_down):
    n, d = x.shape
    ts = min(TS_FFN, n)
    d_ff = w_down.shape[0]
    step = 1024 if d_ff > 1024 else d_ff
    chunks = tuple((lo, min(lo + step, d_ff)) for lo in range(0, d_ff, step))
    x_spec = pl.BlockSpec((ts, d), lambda i: (i, 0))
    return pl.pallas_call(
        functools.partial(_ffn_kernel, chunks, d_ff), grid=(n // ts,),
        in_specs=[x_spec, _const_spec((1, d)), _const_spec(w_gu.shape), _const_spec(w_down.shape)],
        out_specs=x_spec, out_shape=jax.ShapeDtypeStruct((n, d), F32),
        compiler_params=_params(), name="ffn_dense",
    )(x, g, w_gu, w_down)


def _row_copy(src, src_row, dst, dst_row, sem):
    return pltpu.make_async_copy(src.at[pl.ds(src_row, 1)], dst.at[pl.ds(dst_row, 1)], sem)


def _dispatch_kernel(td, tm, starts_ref, ntiles_ref, meta_ref, h_ref, xs_hbm, zero_ref, zsem, sem):
    i = pl.program_id(0)
    zr = zero_ref.shape[0]

    @pl.when(i == 0)
    def _():
        zero_ref[...] = jnp.zeros(zero_ref.shape, F32)

        def zero_copies(e):
            base = pl.multiple_of(starts_ref[e] + (ntiles_ref[e] - 1) * tm, tm)
            return [pltpu.make_async_copy(zero_ref, xs_hbm.at[pl.ds(base + q * zr, zr)], zsem)
                    for q in range(tm // zr)]

        for e in range(N_EXPERTS):
            @pl.when(ntiles_ref[e] > 0)
            def _():
                for cp in zero_copies(e):
                    cp.start()
        for e in range(N_EXPERTS):
            @pl.when(ntiles_ref[e] > 0)
            def _():
                for cp in zero_copies(e):
                    cp.wait()

        n_used = ntiles_ref[0]
        for e in range(1, N_EXPERTS):
            n_used = n_used + ntiles_ref[e]

        def clear_tile(j, carry):
            base = pl.multiple_of(j * tm, tm)
            copies = [pltpu.make_async_copy(zero_ref, xs_hbm.at[pl.ds(base + q * zr, zr)], zsem)
                      for q in range(tm // zr)]
            for cp in copies:
                cp.start()
            for cp in copies:
                cp.wait()
            return carry

        lax.fori_loop(n_used, xs_hbm.shape[0] // tm, clear_tile, 0)

    def issue(t, carry):
        for k in range(2):
            pos = starts_ref[meta_ref[0, 0, 4 * t + k]] + meta_ref[0, 0, 4 * t + 2 + k]
            _row_copy(h_ref, t, xs_hbm, pos, sem).start()
        return carry

    def drain(t, carry):
        for k in range(2):
            _row_copy(h_ref, 0, xs_hbm, 0, sem).wait()
        return carry

    lax.fori_loop(0, td, issue, 0)
    lax.fori_loop(0, td, drain, 0)


def _dispatch(h2, meta, starts, ntiles, r_pad, td, tm):
    n, d = h2.shape
    zr = min(ZR_MOE, tm)
    grid_spec = pltpu.PrefetchScalarGridSpec(
        num_scalar_prefetch=2, grid=(n // td,),
        in_specs=[pl.BlockSpec((1, 1, 4 * td), lambda i, *_: (i, 0, 0), memory_space=pltpu.SMEM),
                  pl.BlockSpec((td, d), lambda i, *_: (i, 0))],
        out_specs=pl.BlockSpec(memory_space=pl.ANY),
        scratch_shapes=[pltpu.VMEM((zr, d), F32), pltpu.SemaphoreType.DMA, pltpu.SemaphoreType.DMA])
    return pl.pallas_call(
        functools.partial(_dispatch_kernel, td, tm), grid_spec=grid_spec,
        out_shape=jax.ShapeDtypeStruct((r_pad, d), F32),
        compiler_params=_params(), name="moe_dispatch",
    )(starts, ntiles, meta, h2)


def _gmm_kernel(te_ref, nu_ref, xs_ref, wg_ref, wu_ref, wd_ref, o_ref):
    i = pl.program_id(0)
    c = pl.program_id(1)

    @pl.when(i < nu_ref[0])
    def _():
        x = xs_ref[...].astype(BF16)
        gate = _dot(x, wg_ref[...])
        up = _dot(x, wu_ref[...])
        a = (gate * _sigmoid(gate) * up).astype(BF16)
        y = _dot(a, wd_ref[...])

        @pl.when(c == 0)
        def _():
            o_ref[...] = y

        @pl.when(c > 0)
        def _():
            o_ref[...] += y

    @pl.when(jnp.logical_and(i >= nu_ref[0], c == 0))
    def _():
        o_ref[...] = jnp.zeros(o_ref.shape, F32)


def _gmm(xs, w_gu, w_down, tile_expert, n_used, tm, fc):
    r_pad, d = xs.shape
    d_ff = w_down.shape[1]
    n_chunks = d_ff // fc
    n_tiles = r_pad // tm

    def tile(i, nu):
        return jnp.minimum(i, nu[0] - 1)

    def chunk(i, c, nu):
        return jnp.where(i < nu[0], c, n_chunks - 1)

    grid_spec = pltpu.PrefetchScalarGridSpec(
        num_scalar_prefetch=2, grid=(n_tiles, n_chunks),
        in_specs=[pl.BlockSpec((tm, d), lambda i, c, te, nu: (tile(i, nu), 0)),
                  pl.BlockSpec((None, d, fc), lambda i, c, te, nu: (te[tile(i, nu)], 0, chunk(i, c, nu))),
                  pl.BlockSpec((None, d, fc),
                               lambda i, c, te, nu: (te[tile(i, nu)], 0, n_chunks + chunk(i, c, nu))),
                  pl.BlockSpec((None, fc, d), lambda i, c, te, nu: (te[tile(i, nu)], chunk(i, c, nu), 0))],
        out_specs=pl.BlockSpec((tm, d), lambda i, c, te, nu: (i, 0)))
    return pl.pallas_call(
        _gmm_kernel, grid_spec=grid_spec, out_shape=jax.ShapeDtypeStruct((r_pad, d), F32),
        compiler_params=_params(2), name="moe_gmm",
    )(tile_expert, n_used, xs, w_gu, w_gu, w_down)


def _combine_kernel(td, final, starts_ref, meta_ref, x_ref, rg_ref, fg_ref, ys_hbm, o_ref, ybuf, sem):
    def issue(t, carry):
        for k in range(2):
            pos = starts_ref[meta_ref[0, 0, 4 * t + k]] + meta_ref[0, 0, 4 * t + 2 + k]
            _row_copy(ys_hbm, pos, ybuf.at[k], t, sem).start()
        return carry

    def drain(t, carry):
        for k in range(2):
            _row_copy(ys_hbm, 0, ybuf.at[k], 0, sem).wait()
        return carry

    lax.fori_loop(0, td, issue, 0)
    lax.fori_loop(0, td, drain, 0)
    out = x_ref[...] + rg_ref[:, 0:1] * ybuf[0] + rg_ref[:, 1:2] * ybuf[1]
    if final:
        out = _rms(out, fg_ref[...])
    o_ref[...] = out


def _combine(x, rg, meta, starts, ys, final_g, final, td):
    n, d = x.shape
    x_spec = pl.BlockSpec((td, d), lambda i, *_: (i, 0))
    grid_spec = pltpu.PrefetchScalarGridSpec(
        num_scalar_prefetch=1, grid=(n // td,),
        in_specs=[pl.BlockSpec((1, 1, 4 * td), lambda i, *_: (i, 0, 0), memory_space=pltpu.SMEM),
                  x_spec,
                  pl.BlockSpec((td, LANES), lambda i, *_: (i, 0)),
                  pl.BlockSpec((1, d), lambda i, *_: (0, 0)),
                  pl.BlockSpec(memory_space=pl.ANY)],
        out_specs=x_spec,
        scratch_shapes=[pltpu.VMEM((2, td, d), F32), pltpu.SemaphoreType.DMA])
    return pl.pallas_call(
        functools.partial(_combine_kernel, td, final), grid_spec=grid_spec,
        out_shape=jax.ShapeDtypeStruct((n, d), F32),
        compiler_params=_params(), name="moe_combine",
    )(starts, meta, x, rg, final_g, ys)


def _moe(x, h2, ri, rg, counts, w_gu, w_down, final_g, final):
    n, d = x.shape
    tm = min(TM_MOE, n)
    td = min(TD_MOE, n)
    fc = min(FC_MOE, w_down.shape[1])
    max_tiles = (2 * n) // tm + N_EXPERTS
    cnt = counts[0, :N_EXPERTS].astype(jnp.int32)
    ntiles = (cnt + tm - 1) // tm
    cum = jnp.cumsum(ntiles)
    starts = (cum - ntiles) * tm
    n_used = cum[-1:]
    tile_ids = jnp.arange(max_tiles, dtype=jnp.int32)
    tile_expert = jnp.minimum(jnp.sum((tile_ids[:, None] >= cum[None, :]).astype(jnp.int32), axis=1),
                              N_EXPERTS - 1)
    meta = ri[:, 0:4].reshape(n // td, 1, 4 * td)
    xs = _dispatch(h2, meta, starts, ntiles, max_tiles * tm, td, tm)
    ys = _gmm(xs, w_gu, w_down, tile_expert, n_used, tm, fc)
    return _combine(x, rg, meta, starts, ys, final_g, final, td)


def _row(v):
    return v.reshape(1, -1).astype(F32)


def _block_diag(pool_w):
    g, c, _ = pool_w.shape
    out = jnp.zeros((g * c, g * c), pool_w.dtype)
    for j in range(g):
        out = lax.dynamic_update_slice(out, pool_w[j], (j * c, j * c))
    return out


def _router_hi_lo(w_router):
    d, e = w_router.shape
    hi = w_router.astype(BF16)
    lo = (w_router - hi.astype(F32)).astype(BF16)
    pad = jnp.zeros((d, LANES - e), BF16)
    return jnp.concatenate([hi, pad, lo, pad], axis=1)


def kernel(x, mem, mix_norm_g, w_in, pool_w, pool_scale, sconv_w, conf_w, conf_b, conf_ln_g, conf_ln_b, group_norm_g, w_out, mem_norm_g, w_mem_kv, xattn_norm_g, w_q, w_o, ffn_norm_g, w_gu_dense, w_down_dense, w_router, w_gu_moe, w_down_moe, final_norm_g):
    b, s, d = x.shape
    m_len = mem.shape[1]
    depth = w_in.shape[0]
    assert depth % 2 == 0, "the last layer must be a routed layer (it applies the final norm)"
    kv = _memkv(mem.reshape(b * m_len, d), _row(mem_norm_g), w_mem_kv.astype(BF16), m_len)
    xf = x.reshape(b * s, d)
    for l in range(depth):
        xf = _mixer(xf, s, _row(mix_norm_g[l]), w_in[l].astype(BF16), _block_diag(pool_w[l]).astype(BF16),
                    _row(pool_scale[l]), sconv_w[l], conf_w[l], _row(conf_b[l]), _row(conf_ln_g[l]),
                    _row(conf_ln_b[l]), _row(group_norm_g[l]), w_out[l].astype(BF16))
        if l % 2 == 0:
            xf = _attn(xf, s, m_len, _row(xattn_norm_g[l]), w_q[l].astype(BF16), kv, w_o[l].astype(BF16))
            xf = _ffn(xf, _row(ffn_norm_g[l]), w_gu_dense[l // 2].astype(BF16), w_down_dense[l // 2].astype(BF16))
        else:
            xf, h2, ri, rg, counts = _attn(
                xf, s, m_len, _row(xattn_norm_g[l]), w_q[l].astype(BF16), kv, w_o[l].astype(BF16),
                route=(_row(ffn_norm_g[l]), _router_hi_lo(w_router[l // 2])))
            xf = _moe(xf, h2, ri, rg, counts, w_gu_moe[l // 2].astype(BF16), w_down_moe[l // 2].astype(BF16),
                      _row(final_norm_g), l == depth - 1)
    return xf.reshape(b, s, d)
```

```python
import functools

import jax
import jax.numpy as jnp
from jax import lax
from jax.experimental import pallas as pl
from jax.experimental.pallas import tpu as pltpu

F32 = jnp.float32
BF16 = jnp.bfloat16
EPS = 1e-6

LANES = 128
SUBLANES = 8
VMEM_LIMIT_BYTES = 56 * 1024 * 1024

POOL_WINDOWS = (2, 4, 8, 16)
N_MEM_HEADS = 4
N_EXPERTS = 8
HALO = 32

TS_MIX = 512
RC_MIX = 64
TS_ATT = 512
TS_FFN = 512
TM_MOE = 1024
FC_MOE = 512
TD_MOE = 1024
ZR_MOE = 256
DMA_UNROLL = 8
STAGE_BYTES = 3 * 1024 * 1024


def _params(n_axes=1):
    return pltpu.CompilerParams(dimension_semantics=("arbitrary",) * n_axes,
                                vmem_limit_bytes=VMEM_LIMIT_BYTES)


def _const_spec(shape):
    nd = len(shape)
    return pl.BlockSpec(shape, lambda *_: (0,) * nd, pipeline_mode=pl.Buffered(1))


_ANY = pl.BlockSpec(memory_space=pl.ANY)


def _rms(x, g):
    ms = jnp.mean(x * x, axis=-1, keepdims=True)
    return x * lax.rsqrt(ms + EPS) * g


def _dot(a, b):
    return jnp.dot(a, b, preferred_element_type=F32)


def _sigmoid(x):
    return 1.0 / (1.0 + jnp.exp(-x))


def _stage_rows(k, n):
    rows = 16
    while k % (2 * rows) == 0 and 2 * rows * n * 4 <= STAGE_BYTES:
        rows *= 2
    assert k % rows == 0
    return rows


def _stage_shape(w):
    k, n = w.shape[-2:]
    return (_stage_rows(k, n), n)


def _load_cast(w_hbm, layer, w_ref, stage_ref, sem):
    rows = stage_ref.shape[0]

    def body(j, carry):
        r = pl.multiple_of(j * rows, rows)
        cp = pltpu.make_async_copy(w_hbm.at[layer, pl.ds(r, rows)], stage_ref, sem)
        cp.start()
        cp.wait()
        w_ref[pl.ds(r, rows), :] = stage_ref[...].astype(BF16)
        return carry

    lax.fori_loop(0, w_ref.shape[0] // rows, body, 0)


def _memkv_kernel(mem_ref, g_ref, w_ref, kv_ref):
    h = _rms(mem_ref[...], g_ref[...]).astype(BF16)
    kv_ref[...] = _dot(h, w_ref[...].astype(BF16)).astype(BF16)


def _memkv(mem2d, g, w_kv, m_len):
    n, d = mem2d.shape
    return pl.pallas_call(
        _memkv_kernel,
        grid=(n // m_len,),
        in_specs=[pl.BlockSpec((m_len, d), lambda i: (i, 0)),
                  _const_spec((1, d)),
                  _const_spec(w_kv.shape)],
        out_specs=pl.BlockSpec((m_len, w_kv.shape[1]), lambda i: (i, 0)),
        out_shape=jax.ShapeDtypeStruct((n, w_kv.shape[1]), BF16),
        compiler_params=_params(),
        name="mem_kv",
    )(mem2d, g, w_kv)


def _mixer_kernel(layer, tiles_per_seq, ts, rc, d_pool, d_sc,
                  x_ref, g_ref, win_hbm, wbd_ref, pscale_ref, sconv_ref, confw_ref, confb_ref,
                  lng_ref, lnb_ref, gng_ref, wout_hbm, o_ref,
                  win_ref, wout_ref, stage_in, stage_out, sem, z_ref, ext_ref, y_ref):
    c0, c1, c2, c3, c4 = d_pool, d_pool + d_sc, d_pool + 2 * d_sc, d_pool + 3 * d_sc, d_pool + 4 * d_sc
    c5 = d_pool + 5 * d_sc
    d_mix = d_pool + 2 * d_sc
    n_taps = confw_ref.shape[0]
    grp = d_pool // len(POOL_WINDOWS)
    i = pl.program_id(0)
    tile_in_seq = i % tiles_per_seq

    @pl.when(i == 0)
    def _():
        _load_cast(win_hbm, layer, win_ref, stage_in, sem)
        _load_cast(wout_hbm, layer, wout_ref, stage_out, sem)

    @pl.when(tile_in_seq == 0)
    def _():
        ext_ref[0:HALO, :] = jnp.zeros((HALO, d_mix), F32)

    h = _rms(x_ref[...], g_ref[...]).astype(BF16)
    z_ref[...] = _dot(h, win_ref[...])

    lane = lax.broadcasted_iota(jnp.int32, (rc, d_pool), 1)
    row = lax.broadcasted_iota(jnp.int32, (rc, d_pool), 0)
    win = jnp.where(lane < grp, POOL_WINDOWS[0],
                    jnp.where(lane < 2 * grp, POOL_WINDOWS[1],
                              jnp.where(lane < 3 * grp, POOL_WINDOWS[2], POOL_WINDOWS[3])))

    def chunk(c, carry):
        r0 = pl.multiple_of(c * rc, rc)

        def zc(lo, hi):
            return z_ref[pl.ds(r0, rc), lo:hi]

        ext_ref[pl.ds(HALO + r0, rc), 0:c0] = zc(0, c0)
        ext_ref[pl.ds(HALO + r0, rc), c0:c1] = zc(c2, c3) * zc(c0, c1)
        ext_ref[pl.ds(HALO + r0, rc), c1:d_mix] = zc(c3, c4) * _sigmoid(zc(c4, c5))

        e = ext_ref[pl.ds(r0 + 8, rc + 24), 0:c0]
        s2 = e + pltpu.roll(e, 1, 0)
        s4 = s2 + pltpu.roll(s2, 2, 0)
        s8 = s4 + pltpu.roll(s4, 4, 0)
        s16 = s8[24:] + s8[16:16 + rc]
        ssum = jnp.where(lane < grp, s2[24:],
                         jnp.where(lane < 2 * grp, s4[24:],
                                   jnp.where(lane < 3 * grp, s8[24:], s16)))
        tpos = tile_in_seq * ts + r0 + row
        cnt = jnp.minimum(tpos + 1, win).astype(F32)
        pooled = ssum / cnt - e[24:]
        ya = _dot(pooled.astype(BF16), wbd_ref[...]) * pscale_ref[...]
        y_ref[pl.ds(r0, rc), 0:c0] = _rms(ya, gng_ref[:, 0:c0]).astype(BF16)

        sx = ext_ref[pl.ds(r0 + 24, rc + 8), c0:c1]
        conv = (sx[8:] * sconv_ref[2:3, :]
                + pltpu.roll(sx, 1, 0)[8:] * sconv_ref[1:2, :]
                + pltpu.roll(sx, 2, 0)[8:] * sconv_ref[0:1, :])
        yb = zc(c1, c2) * conv
        y_ref[pl.ds(r0, rc), c0:c1] = _rms(yb, gng_ref[:, c0:c1]).astype(BF16)

        acc = None
        for b in range(SUBLANES):
            pb = None
            for a in range((n_taps + SUBLANES - 1) // SUBLANES):
                d = SUBLANES * a + b
                if d >= n_taps:
                    continue
                xa = ext_ref[pl.ds(r0 + 24 - SUBLANES * a, rc + 8), c1:d_mix]
                k = n_taps - 1 - d
                term = xa * confw_ref[k:k + 1, :]
                pb = term if pb is None else pb + term
            if b:
                pb = pltpu.roll(pb, b, 0)
            acc = pb[8:] if acc is None else acc + pb[8:]
        hc = acc + confb_ref[...]
        mu = jnp.mean(hc, axis=-1, keepdims=True)
        xc = hc - mu
        var = jnp.mean(xc * xc, axis=-1, keepdims=True)
        ln = xc * lax.rsqrt(var + EPS) * lng_ref[...] + lnb_ref[...]
        yc = ln * _sigmoid(ln)
        y_ref[pl.ds(r0, rc), c1:d_mix] = _rms(yc, gng_ref[:, c1:d_mix]).astype(BF16)
        return carry

    lax.fori_loop(0, ts // rc, chunk, 0)

    o_ref[...] = x_ref[...] + _dot(y_ref[...], wout_ref[...])
    ext_ref[0:HALO, :] = ext_ref[ts:ts + HALO, :]


def _mixer(x, layer, seq_len, g, w_in, w_bd, pscale, sconv_w, conf_w, conf_b, ln_g, ln_b, gn_g, w_out):
    n, d = x.shape
    ts = min(TS_MIX, seq_len)
    rc = min(RC_MIX, ts)
    d_pool = w_bd.shape[0]
    d_sc = sconv_w.shape[1]
    d_in = w_in.shape[2]
    d_mix = w_out.shape[1]
    assert conf_w.shape[0] <= HALO - 1 and seq_len % ts == 0 and ts % rc == 0
    kern = functools.partial(_mixer_kernel, layer, seq_len // ts, ts, rc, d_pool, d_sc)
    small = (w_bd, pscale, sconv_w, conf_w, conf_b, ln_g, ln_b, gn_g)
    return pl.pallas_call(
        kern,
        grid=(n // ts,),
        in_specs=([pl.BlockSpec((ts, d), lambda i: (i, 0)), _const_spec(g.shape), _ANY]
                  + [_const_spec(c.shape) for c in small] + [_ANY]),
        out_specs=pl.BlockSpec((ts, d), lambda i: (i, 0)),
        out_shape=jax.ShapeDtypeStruct((n, d), F32),
        scratch_shapes=[pltpu.VMEM((d, d_in), BF16), pltpu.VMEM((d_mix, d), BF16),
                        pltpu.VMEM(_stage_shape(w_in), F32), pltpu.VMEM(_stage_shape(w_out), F32),
                        pltpu.SemaphoreType.DMA,
                        pltpu.VMEM((ts, d_in), F32),
                        pltpu.VMEM((HALO + ts, d_mix), F32),
                        pltpu.VMEM((ts, d_mix), BF16)],
        compiler_params=_params(),
        name="mixer",
    )(x, g, w_in, *small, w_out)


def _attn_body(layer, x_ref, g_ref, wq_hbm, k_ref, v_ref, wo_hbm, wq_ref, wo_ref, stage_ref, sem):
    @pl.when(pl.program_id(0) == 0)
    def _():
        _load_cast(wq_hbm, layer, wq_ref, stage_ref, sem)
        _load_cast(wo_hbm, layer, wo_ref, stage_ref, sem)

    d = x_ref.shape[1]
    dh = d // N_MEM_HEADS
    x = x_ref[...]
    h = _rms(x, g_ref[...]).astype(BF16)
    q = _dot(h, wq_ref[...])
    outs = []
    for hh in range(N_MEM_HEADS):
        sl = slice(hh * dh, (hh + 1) * dh)
        s = lax.dot_general(q[:, sl].astype(BF16), k_ref[:, sl], (((1,), (1,)), ((), ())),
                            preferred_element_type=F32) * (dh ** -0.5)
        e = jnp.exp(s - jnp.max(s, axis=-1, keepdims=True))
        p = e * (1.0 / jnp.sum(e, axis=-1, keepdims=True))
        outs.append(_dot(p.astype(BF16), v_ref[:, sl]).astype(BF16))
    return x + _dot(jnp.concatenate(outs, axis=-1), wo_ref[...])


def _attn_kernel(layer, x_ref, g_ref, wq_hbm, k_ref, v_ref, wo_hbm, o_ref, wq_ref, wo_ref, stage_ref, sem):
    o_ref[...] = _attn_body(layer, x_ref, g_ref, wq_hbm, k_ref, v_ref, wo_hbm, wq_ref, wo_ref, stage_ref, sem)


def _attn_route_kernel(layer, x_ref, g_ref, wq_hbm, k_ref, v_ref, wo_hbm, fg_ref, wr_ref, tri_ref,
                       o_ref, h2_ref, ri_ref, rg_ref, cnt_ref, wq_ref, wo_ref, stage_ref, sem, carry_ref):
    i = pl.program_id(0)

    @pl.when(i == 0)
    def _():
        carry_ref[...] = jnp.zeros(carry_ref.shape, F32)

    xn = _attn_body(layer, x_ref, g_ref, wq_hbm, k_ref, v_ref, wo_hbm, wq_ref, wo_ref, stage_ref, sem)
    o_ref[...] = xn
    h2 = _rms(xn, fg_ref[...])
    h2_ref[...] = h2

    h_hi = h2.astype(BF16)
    h_lo = (h2 - h_hi.astype(F32)).astype(BF16)
    part = _dot(h_hi, wr_ref[...])
    logits = part[:, 0:LANES] + part[:, LANES:2 * LANES] + _dot(h_lo, wr_ref[:, 0:LANES])

    ts = x_ref.shape[0]
    lane = lax.broadcasted_iota(jnp.int32, (ts, LANES), 1).astype(F32)
    neg = jnp.float32(-jnp.inf)
    lg = jnp.where(lane < N_EXPERTS, logits, neg)
    v1 = jnp.max(lg, axis=-1, keepdims=True)
    i1 = jnp.min(jnp.where(lg == v1, lane, float(LANES)), axis=-1, keepdims=True)
    lg2 = jnp.where(lane == i1, neg, lg)
    v2 = jnp.max(lg2, axis=-1, keepdims=True)
    i2 = jnp.min(jnp.where(lg2 == v2, lane, float(LANES)), axis=-1, keepdims=True)
    e2 = jnp.exp(v2 - v1)
    den = 1.0 + e2
    w1 = 1.0 / den
    w2 = e2 / den

    sel1 = lane == i1
    sel2 = lane == i2
    onehot = jnp.where(sel1 | sel2, 1.0, 0.0)
    ranks = carry_ref[0:1, :] + _dot(tri_ref[...], onehot.astype(BF16))
    r1 = jnp.sum(jnp.where(sel1, ranks, 0.0), axis=-1, keepdims=True)
    r2 = jnp.sum(jnp.where(sel2, ranks, 0.0), axis=-1, keepdims=True)
    total = carry_ref[0:1, :] + jnp.sum(onehot, axis=0, keepdims=True)
    carry_ref[...] = jnp.broadcast_to(total, carry_ref.shape)
    cnt_ref[...] = jnp.broadcast_to(total, cnt_ref.shape)

    meta = jnp.where(lane == 0, i1, jnp.where(lane == 1, i2, jnp.where(lane == 2, r1, jnp.where(lane == 3, r2, 0.0))))
    ri_ref[...] = meta.astype(jnp.int32)
    rg_ref[...] = jnp.where(lane == 0, w1, jnp.where(lane == 1, w2, 0.0))


def _attn(x, layer, seq_len, m_len, g, w_q, kv, w_o, route=None):
    n, d = x.shape
    ts = min(TS_ATT, seq_len)
    tps = seq_len // ts
    x_spec = pl.BlockSpec((ts, d), lambda i: (i, 0))
    in_specs = [x_spec, _const_spec((1, d)), _ANY,
                pl.BlockSpec((m_len, d), lambda i: (i // tps, 0)),
                pl.BlockSpec((m_len, d), lambda i: (i // tps, 1)),
                _ANY]
    args = [x, g, w_q, kv, kv, w_o]
    w_scratch = [pltpu.VMEM((d, d), BF16), pltpu.VMEM((d, d), BF16),
                 pltpu.VMEM(_stage_shape(w_q), F32), pltpu.SemaphoreType.DMA]
    if route is None:
        return pl.pallas_call(
            functools.partial(_attn_kernel, layer), grid=(n // ts,), in_specs=in_specs, out_specs=x_spec,
            out_shape=jax.ShapeDtypeStruct((n, d), F32), scratch_shapes=w_scratch,
            compiler_params=_params(), name="attn",
        )(*args)
    ffn_g, w_r2 = route
    tri = jnp.tril(jnp.ones((ts, ts), BF16), -1)
    lane_spec = pl.BlockSpec((ts, LANES), lambda i: (i, 0))
    return pl.pallas_call(
        functools.partial(_attn_route_kernel, layer), grid=(n // ts,),
        in_specs=in_specs + [_const_spec((1, d)), _const_spec(w_r2.shape), _const_spec(tri.shape)],
        out_specs=[x_spec, x_spec, lane_spec, lane_spec,
                   pl.BlockSpec((SUBLANES, LANES), lambda i: (0, 0))],
        out_shape=[jax.ShapeDtypeStruct((n, d), F32), jax.ShapeDtypeStruct((n, d), F32),
                   jax.ShapeDtypeStruct((n, LANES), jnp.int32), jax.ShapeDtypeStruct((n, LANES), F32),
                   jax.ShapeDtypeStruct((SUBLANES, LANES), F32)],
        scratch_shapes=w_scratch + [pltpu.VMEM((SUBLANES, LANES), F32)],
        compiler_params=_params(), name="attn_route",
    )(*args, ffn_g, w_r2, tri)


def _ffn_kernel(layer, chunks, d_ff, x_ref, g_ref, wgu_hbm, wd_hbm, o_ref,
                wgu_ref, wd_ref, stage_gu, stage_d, sem):
    @pl.when(pl.program_id(0) == 0)
    def _():
        _load_cast(wgu_hbm, layer, wgu_ref, stage_gu, sem)
        _load_cast(wd_hbm, layer, wd_ref, stage_d, sem)

    x = x_ref[...]
    h = _rms(x, g_ref[...]).astype(BF16)
    acc = x
    for lo, hi in chunks:
        gate = _dot(h, wgu_ref[:, lo:hi])
        up = _dot(h, wgu_ref[:, d_ff + lo:d_ff + hi])
        a = (gate * _sigmoid(gate) * up).astype(BF16)
        acc = acc + _dot(a, wd_ref[lo:hi, :])
    o_ref[...] = acc


def _ffn(x, layer, g, w_gu, w_down):
    n, d = x.shape
    ts = min(TS_FFN, n)
    d_ff = w_down.shape[1]
    step = 1024 if d_ff > 1024 else d_ff
    chunks = tuple((lo, min(lo + step, d_ff)) for lo in range(0, d_ff, step))
    x_spec = pl.BlockSpec((ts, d), lambda i: (i, 0))
    return pl.pallas_call(
        functools.partial(_ffn_kernel, layer, chunks, d_ff), grid=(n // ts,),
        in_specs=[x_spec, _const_spec((1, d)), _ANY, _ANY],
        out_specs=x_spec, out_shape=jax.ShapeDtypeStruct((n, d), F32),
        scratch_shapes=[pltpu.VMEM(w_gu.shape[1:], BF16), pltpu.VMEM(w_down.shape[1:], BF16),
                        pltpu.VMEM(_stage_shape(w_gu), F32), pltpu.VMEM(_stage_shape(w_down), F32),
                        pltpu.SemaphoreType.DMA],
        compiler_params=_params(), name="ffn_dense",
    )(x, g, w_gu, w_down)


def _row_copy(src, src_row, dst, dst_row, sem):
    return pltpu.make_async_copy(src.at[pl.ds(src_row, 1)], dst.at[pl.ds(dst_row, 1)], sem)


def _dispatch_kernel(td, tm, starts_ref, ntiles_ref, pos_ref, h_ref, xs_hbm, zero_ref, zsem, sem):
    i = pl.program_id(0)
    zr = zero_ref.shape[0]

    @pl.when(i == 0)
    def _():
        zero_ref[...] = jnp.zeros(zero_ref.shape, F32)

        def zero_copies(e):
            base = pl.multiple_of(starts_ref[e] + (ntiles_ref[e] - 1) * tm, tm)
            return [pltpu.make_async_copy(zero_ref, xs_hbm.at[pl.ds(base + q * zr, zr)], zsem)
                    for q in range(tm // zr)]

        for e in range(N_EXPERTS):
            @pl.when(ntiles_ref[e] > 0)
            def _():
                for cp in zero_copies(e):
                    cp.start()
        for e in range(N_EXPERTS):
            @pl.when(ntiles_ref[e] > 0)
            def _():
                for cp in zero_copies(e):
                    cp.wait()

        n_used = ntiles_ref[0]
        for e in range(1, N_EXPERTS):
            n_used = n_used + ntiles_ref[e]

        def clear_tile(j, carry):
            base = pl.multiple_of(j * tm, tm)
            copies = [pltpu.make_async_copy(zero_ref, xs_hbm.at[pl.ds(base + q * zr, zr)], zsem)
                      for q in range(tm // zr)]
            for cp in copies:
                cp.start()
            for cp in copies:
                cp.wait()
            return carry

        lax.fori_loop(n_used, xs_hbm.shape[0] // tm, clear_tile, 0)

    def issue(t, carry):
        for k in range(2):
            _row_copy(h_ref, t, xs_hbm, pos_ref[0, 0, 2 * t + k], sem).start()
        return carry

    def drain(t, carry):
        for k in range(2):
            _row_copy(h_ref, 0, xs_hbm, 0, sem).wait()
        return carry

    lax.fori_loop(0, td, issue, 0, unroll=DMA_UNROLL)
    lax.fori_loop(0, td, drain, 0, unroll=DMA_UNROLL)


def _dispatch(h2, pos, starts, ntiles, r_pad, td, tm):
    n, d = h2.shape
    zr = min(ZR_MOE, tm)
    grid_spec = pltpu.PrefetchScalarGridSpec(
        num_scalar_prefetch=2, grid=(n // td,),
        in_specs=[pl.BlockSpec((1, 1, 2 * td), lambda i, *_: (i, 0, 0), memory_space=pltpu.SMEM),
                  pl.BlockSpec((td, d), lambda i, *_: (i, 0))],
        out_specs=_ANY,
        scratch_shapes=[pltpu.VMEM((zr, d), F32), pltpu.SemaphoreType.DMA, pltpu.SemaphoreType.DMA])
    return pl.pallas_call(
        functools.partial(_dispatch_kernel, td, tm), grid_spec=grid_spec,
        out_shape=jax.ShapeDtypeStruct((r_pad, d), F32),
        compiler_params=_params(), name="moe_dispatch",
    )(starts, ntiles, pos, h2)


def _gmm_kernel(te_ref, nu_ref, xs_ref, wg_ref, wu_ref, wd_ref, o_ref, xb_ref):
    i = pl.program_id(0)
    c = pl.program_id(1)

    @pl.when(i < nu_ref[0])
    def _():
        @pl.when(c == 0)
        def _():
            xb_ref[...] = xs_ref[...].astype(BF16)
            o_ref[...] = jnp.zeros(o_ref.shape, F32)

        x = xb_ref[...]
        gate = _dot(x, wg_ref[...].astype(BF16))
        up = _dot(x, wu_ref[...].astype(BF16))
        a = (gate * _sigmoid(gate) * up).astype(BF16)
        o_ref[...] += _dot(a, wd_ref[...].astype(BF16))

    @pl.when(jnp.logical_and(i >= nu_ref[0], c == 0))
    def _():
        o_ref[...] = jnp.zeros(o_ref.shape, F32)


def _gmm(xs, layer, w_gu, w_down, tile_expert, n_used, tm, fc):
    r_pad, d = xs.shape
    d_ff = w_down.shape[2]
    n_chunks = d_ff // fc
    n_tiles = r_pad // tm

    def tile(i, nu):
        return jnp.minimum(i, nu[0] - 1)

    def chunk(i, c, nu):
        return jnp.where(i < nu[0], c, n_chunks - 1)

    grid_spec = pltpu.PrefetchScalarGridSpec(
        num_scalar_prefetch=2, grid=(n_tiles, n_chunks),
        in_specs=[pl.BlockSpec((tm, d), lambda i, c, te, nu: (tile(i, nu), 0)),
                  pl.BlockSpec((None, None, d, fc),
                               lambda i, c, te, nu: (layer, te[tile(i, nu)], 0, chunk(i, c, nu))),
                  pl.BlockSpec((None, None, d, fc),
                               lambda i, c, te, nu: (layer, te[tile(i, nu)], 0, n_chunks + chunk(i, c, nu))),
                  pl.BlockSpec((None, None, fc, d),
                               lambda i, c, te, nu: (layer, te[tile(i, nu)], chunk(i, c, nu), 0))],
        out_specs=pl.BlockSpec((tm, d), lambda i, c, te, nu: (i, 0)),
        scratch_shapes=[pltpu.VMEM((tm, d), BF16)])
    return pl.pallas_call(
        _gmm_kernel, grid_spec=grid_spec, out_shape=jax.ShapeDtypeStruct((r_pad, d), F32),
        compiler_params=_params(2), name="moe_gmm",
    )(tile_expert, n_used, xs, w_gu, w_gu, w_down)


def _combine_kernel(td, final, pos_ref, x_ref, rg_ref, fg_ref, ys_hbm, o_ref, ybuf, sem):
    def issue(t, carry):
        for k in range(2):
            _row_copy(ys_hbm, pos_ref[0, 0, 2 * t + k], ybuf.at[k], t, sem).start()
        return carry

    def drain(t, carry):
        for k in range(2):
            _row_copy(ys_hbm, 0, ybuf.at[k], 0, sem).wait()
        return carry

    lax.fori_loop(0, td, issue, 0, unroll=DMA_UNROLL)
    lax.fori_loop(0, td, drain, 0, unroll=DMA_UNROLL)
    out = x_ref[...] + rg_ref[:, 0:1] * ybuf[0] + rg_ref[:, 1:2] * ybuf[1]
    if final:
        out = _rms(out, fg_ref[...])
    o_ref[...] = out


def _combine(x, rg, pos, ys, final_g, final, td):
    n, d = x.shape
    x_spec = pl.BlockSpec((td, d), lambda i: (i, 0))
    return pl.pallas_call(
        functools.partial(_combine_kernel, td, final), grid=(n // td,),
        in_specs=[pl.BlockSpec((1, 1, 2 * td), lambda i: (i, 0, 0), memory_space=pltpu.SMEM),
                  x_spec,
                  pl.BlockSpec((td, LANES), lambda i: (i, 0)),
                  pl.BlockSpec((1, d), lambda i: (0, 0)),
                  _ANY],
        out_specs=x_spec,
        out_shape=jax.ShapeDtypeStruct((n, d), F32),
        scratch_shapes=[pltpu.VMEM((2, td, d), F32), pltpu.SemaphoreType.DMA],
        compiler_params=_params(), name="moe_combine",
    )(pos, x, rg, final_g, ys)


def _moe(x, layer, h2, ri, rg, counts, w_gu, w_down, final_g, final):
    n, d = x.shape
    tm = min(TM_MOE, n)
    td = min(TD_MOE, n)
    fc = min(FC_MOE, w_down.shape[2])
    max_tiles = (2 * n) // tm + N_EXPERTS
    cnt = counts[0, :N_EXPERTS].astype(jnp.int32)
    ntiles = (cnt + tm - 1) // tm
    cum = jnp.cumsum(ntiles)
    starts = (cum - ntiles) * tm
    n_used = cum[-1:]
    tile_ids = jnp.arange(max_tiles, dtype=jnp.int32)
    tile_expert = jnp.minimum(jnp.sum((tile_ids[:, None] >= cum[None, :]).astype(jnp.int32), axis=1),
                              N_EXPERTS - 1)
    start_of = jnp.sum(jnp.where(ri[:, 0:2, None] == jnp.arange(N_EXPERTS, dtype=jnp.int32), starts, 0), axis=-1)
    pos = (start_of + ri[:, 2:4]).reshape(n // td, 1, 2 * td)
    xs = _dispatch(h2, pos, starts, ntiles, max_tiles * tm, td, tm)
    ys = _gmm(xs, layer, w_gu, w_down, tile_expert, n_used, tm, fc)
    return _combine(x, rg, pos, ys, final_g, final, td)


def _row(v):
    return v.reshape(1, -1).astype(F32)


def _block_diag(pool_w):
    g, c, _ = pool_w.shape
    out = jnp.zeros((g * c, g * c), pool_w.dtype)
    for j in range(g):
        out = lax.dynamic_update_slice(out, pool_w[j], (j * c, j * c))
    return out


def _router_hi_lo(w_router):
    d, e = w_router.shape
    hi = w_router.astype(BF16)
    lo = (w_router - hi.astype(F32)).astype(BF16)
    pad = jnp.zeros((d, LANES - e), BF16)
    return jnp.concatenate([hi, pad, lo, pad], axis=1)


def kernel(x, mem, mix_norm_g, w_in, pool_w, pool_scale, sconv_w, conf_w, conf_b, conf_ln_g, conf_ln_b, group_norm_g, w_out, mem_norm_g, w_mem_kv, xattn_norm_g, w_q, w_o, ffn_norm_g, w_gu_dense, w_down_dense, w_router, w_gu_moe, w_down_moe, final_norm_g):
    b, s, d = x.shape
    m_len = mem.shape[1]
    depth = w_in.shape[0]
    assert depth % 2 == 0, "the last layer must be a routed layer (it applies the final norm)"
    kv = _memkv(mem.reshape(b * m_len, d), _row(mem_norm_g), w_mem_kv, m_len)
    xf = x.reshape(b * s, d)
    for l in range(depth):
        xf = _mixer(xf, l, s, _row(mix_norm_g[l]), w_in, _block_diag(pool_w[l]).astype(BF16),
                    _row(pool_scale[l]), sconv_w[l], conf_w[l], _row(conf_b[l]), _row(conf_ln_g[l]),
                    _row(conf_ln_b[l]), _row(group_norm_g[l]), w_out)
        if l % 2 == 0:
            xf = _attn(xf, l, s, m_len, _row(xattn_norm_g[l]), w_q, kv, w_o)
            xf = _ffn(xf, l // 2, _row(ffn_norm_g[l]), w_gu_dense, w_down_dense)
        else:
            xf, h2, ri, rg, counts = _attn(
                xf, l, s, m_len, _row(xattn_norm_g[l]), w_q, kv, w_o,
                route=(_row(ffn_norm_g[l]), _router_hi_lo(w_router[l // 2])))
            xf = _moe(xf, l // 2, h2, ri, rg, counts, w_gu_moe, w_down_moe, _row(final_norm_g), l == depth - 1)
    return xf.reshape(b, s, d)
```

```python
import functools

import jax
import jax.numpy as jnp
from jax import lax
from jax.experimental import pallas as pl
from jax.experimental.pallas import tpu as pltpu

F32 = jnp.float32
BF16 = jnp.bfloat16
EPS = 1e-6

LANES = 128
SUBLANES = 8
VMEM_LIMIT_BYTES = 56 * 1024 * 1024

POOL_WINDOWS = (2, 4, 8, 16)
N_MEM_HEADS = 4
N_EXPERTS = 8
HALO = 32

TS_MIX = 512
RC_MIX = 64
TS_ATT = 1024
TS_FFN = 512
TM_MOE = 1024
FC_MOE = 512
TD_MOE = 1024
ZR_MOE = 256
DMA_UNROLL = 8
STAGE_BYTES = 3 * 1024 * 1024


def _params(n_axes=1):
    return pltpu.CompilerParams(dimension_semantics=("arbitrary",) * n_axes,
                                vmem_limit_bytes=VMEM_LIMIT_BYTES)


def _const_spec(shape):
    nd = len(shape)
    return pl.BlockSpec(shape, lambda *_: (0,) * nd, pipeline_mode=pl.Buffered(1))


_ANY = pl.BlockSpec(memory_space=pl.ANY)


def _rms(x, g):
    ms = jnp.mean(x * x, axis=-1, keepdims=True)
    return x * lax.rsqrt(ms + EPS) * g


def _dot(a, b):
    return jnp.dot(a, b, preferred_element_type=F32)


def _sigmoid(x):
    return 1.0 / (1.0 + jnp.exp(-x))


def _stage_rows(k, n):
    rows = 16
    while k % (2 * rows) == 0 and 2 * rows * n * 4 <= STAGE_BYTES:
        rows *= 2
    assert k % rows == 0
    return rows


def _stage_shape(w):
    k, n = w.shape[-2:]
    return (_stage_rows(k, n), n)


def _load_cast(w_hbm, layer, w_ref, stage_ref, sem):
    rows = stage_ref.shape[0]

    def body(j, carry):
        r = pl.multiple_of(j * rows, rows)
        cp = pltpu.make_async_copy(w_hbm.at[layer, pl.ds(r, rows)], stage_ref, sem)
        cp.start()
        cp.wait()
        w_ref[pl.ds(r, rows), :] = stage_ref[...].astype(BF16)
        return carry

    lax.fori_loop(0, w_ref.shape[0] // rows, body, 0)


def _memkv_kernel(mem_ref, g_ref, w_ref, kv_ref):
    h = _rms(mem_ref[...], g_ref[...]).astype(BF16)
    kv_ref[...] = _dot(h, w_ref[...].astype(BF16)).astype(BF16)


def _memkv(mem2d, g, w_kv, m_len):
    n, d = mem2d.shape
    return pl.pallas_call(
        _memkv_kernel,
        grid=(n // m_len,),
        in_specs=[pl.BlockSpec((m_len, d), lambda i: (i, 0)),
                  _const_spec((1, d)),
                  _const_spec(w_kv.shape)],
        out_specs=pl.BlockSpec((m_len, w_kv.shape[1]), lambda i: (i, 0)),
        out_shape=jax.ShapeDtypeStruct((n, w_kv.shape[1]), BF16),
        compiler_params=_params(),
        name="mem_kv",
    )(mem2d, g, w_kv)


def _mixer_kernel(layer, tiles_per_seq, ts, rc, d_pool, d_sc,
                  xa_ref, xc_ref, g_ref, win_hbm, wbd_ref, pscale_ref, sconv_ref, confw_ref, confb_ref,
                  lng_ref, lnb_ref, gng_ref, wout_hbm, o_ref,
                  win_ref, wout_ref, stage_in, stage_out, sem, z0_ref, z1_ref, ext_ref, y0_ref, y1_ref):
    c0, c1, c2, c3, c4 = d_pool, d_pool + d_sc, d_pool + 2 * d_sc, d_pool + 3 * d_sc, d_pool + 4 * d_sc
    c5 = d_pool + 5 * d_sc
    d_mix = d_pool + 2 * d_sc
    n_taps = confw_ref.shape[0]
    grp = d_pool // len(POOL_WINDOWS)
    j = pl.program_id(0)

    @pl.when(j == 0)
    def _():
        _load_cast(win_hbm, layer, win_ref, stage_in, sem)
        _load_cast(wout_hbm, layer, wout_ref, stage_out, sem)
        z1_ref[...] = jnp.zeros(z1_ref.shape, F32)
        y0_ref[...] = jnp.zeros(y0_ref.shape, BF16)
        ext_ref[...] = jnp.zeros(ext_ref.shape, F32)

    lane = lax.broadcasted_iota(jnp.int32, (rc, d_pool), 1)
    row = lax.broadcasted_iota(jnp.int32, (rc, d_pool), 0)
    win = jnp.where(lane < grp, POOL_WINDOWS[0],
                    jnp.where(lane < 2 * grp, POOL_WINDOWS[1],
                              jnp.where(lane < 3 * grp, POOL_WINDOWS[2], POOL_WINDOWS[3])))

    def stage_a(rows, z_ref):
        h = _rms(xa_ref[rows, :], g_ref[...]).astype(BF16)
        z_ref[...] = _dot(h, win_ref[...])

    def stage_c(rows, y_ref):
        o_ref[rows, :] = xc_ref[rows, :] + _dot(y_ref[...], wout_ref[...])

    def chunk(c, tile_in_seq, z_ref, y_ref):
        r0 = c * rc

        def zc(lo, hi):
            return z_ref[pl.ds(r0, rc), lo:hi]

        ext_ref[pl.ds(HALO + r0, rc), 0:c0] = zc(0, c0)
        ext_ref[pl.ds(HALO + r0, rc), c0:c1] = zc(c2, c3) * zc(c0, c1)
        ext_ref[pl.ds(HALO + r0, rc), c1:d_mix] = zc(c3, c4) * _sigmoid(zc(c4, c5))

        e = ext_ref[pl.ds(r0 + 8, rc + 24), 0:c0]
        s2 = e + pltpu.roll(e, 1, 0)
        s4 = s2 + pltpu.roll(s2, 2, 0)
        s8 = s4 + pltpu.roll(s4, 4, 0)
        s16 = s8[24:] + s8[16:16 + rc]
        ssum = jnp.where(lane < grp, s2[24:],
                         jnp.where(lane < 2 * grp, s4[24:],
                                   jnp.where(lane < 3 * grp, s8[24:], s16)))
        tpos = tile_in_seq * ts + r0 + row
        cnt = jnp.minimum(tpos + 1, win).astype(F32)
        pooled = ssum / cnt - e[24:]
        ya = _dot(pooled.astype(BF16), wbd_ref[...]) * pscale_ref[...]
        y_ref[pl.ds(r0, rc), 0:c0] = _rms(ya, gng_ref[:, 0:c0]).astype(BF16)

        sx = ext_ref[pl.ds(r0 + 24, rc + 8), c0:c1]
        conv = (sx[8:] * sconv_ref[2:3, :]
                + pltpu.roll(sx, 1, 0)[8:] * sconv_ref[1:2, :]
                + pltpu.roll(sx, 2, 0)[8:] * sconv_ref[0:1, :])
        yb = zc(c1, c2) * conv
        y_ref[pl.ds(r0, rc), c0:c1] = _rms(yb, gng_ref[:, c0:c1]).astype(BF16)

        hcs = []
        for wl in range(0, d_sc, LANES):
            wh = wl + LANES
            acc = None
            for b in range(SUBLANES):
                pb = None
                for a in range((n_taps + SUBLANES - 1) // SUBLANES):
                    d = SUBLANES * a + b
                    if d >= n_taps:
                        continue
                    xa = ext_ref[pl.ds(r0 + 24 - SUBLANES * a, rc + 8), c1 + wl:c1 + wh]
                    k = n_taps - 1 - d
                    term = xa * confw_ref[k:k + 1, wl:wh]
                    pb = term if pb is None else pb + term
                if b:
                    pb = pltpu.roll(pb, b, 0)
                acc = pb[8:] if acc is None else acc + pb[8:]
            hcs.append(acc + confb_ref[:, wl:wh])
        hc = jnp.concatenate(hcs, axis=-1)
        mu = jnp.mean(hc, axis=-1, keepdims=True)
        xc = hc - mu
        var = jnp.mean(xc * xc, axis=-1, keepdims=True)
        ln = xc * lax.rsqrt(var + EPS) * lng_ref[...] + lnb_ref[...]
        yc = ln * _sigmoid(ln)
        y_ref[pl.ds(r0, rc), c1:d_mix] = _rms(yc, gng_ref[:, c1:d_mix]).astype(BF16)

    def stage_b(tile, z_ref, y_ref):
        tile_in_seq = (tile + tiles_per_seq) % tiles_per_seq
        ext_ref[0:HALO, :] = jnp.where(tile_in_seq == 0, 0.0, ext_ref[0:HALO, :])
        for c in range(ts // rc):
            chunk(c, tile_in_seq, z_ref, y_ref)
        ext_ref[0:HALO, :] = ext_ref[ts:ts + HALO, :]

    lo_rows, hi_rows = slice(0, ts), slice(ts, 2 * ts)
    stage_c(lo_rows, y0_ref)
    stage_a(lo_rows, z0_ref)
    stage_b(2 * j - 1, z1_ref, y1_ref)
    stage_a(hi_rows, z1_ref)
    stage_b(2 * j, z0_ref, y0_ref)
    stage_c(hi_rows, y1_ref)


def _mixer(x, layer, seq_len, g, w_in, w_bd, pscale, sconv_w, conf_w, conf_b, ln_g, ln_b, gn_g, w_out):
    n, d = x.shape
    ts = min(TS_MIX, seq_len)
    rc = min(RC_MIX, ts)
    n_pairs = n // (2 * ts)
    d_pool = w_bd.shape[0]
    d_sc = sconv_w.shape[1]
    d_in = w_in.shape[2]
    d_mix = w_out.shape[1]
    assert conf_w.shape[0] <= HALO - 1 and seq_len % ts == 0 and ts % rc == 0 and n % (2 * ts) == 0
    kern = functools.partial(_mixer_kernel, layer, seq_len // ts, ts, rc, d_pool, d_sc)
    small = (w_bd, pscale, sconv_w, conf_w, conf_b, ln_g, ln_b, gn_g)
    last = n_pairs - 1
    return pl.pallas_call(
        kern,
        grid=(n_pairs + 1,),
        in_specs=([pl.BlockSpec((2 * ts, d), lambda j: (jnp.minimum(j, last), 0)),
                   pl.BlockSpec((2 * ts, d), lambda j: (jnp.maximum(j - 1, 0), 0)),
                   _const_spec(g.shape), _ANY]
                  + [_const_spec(c.shape) for c in small] + [_ANY]),
        out_specs=pl.BlockSpec((2 * ts, d), lambda j: (jnp.maximum(j - 1, 0), 0)),
        out_shape=jax.ShapeDtypeStruct((n, d), F32),
        scratch_shapes=[pltpu.VMEM((d, d_in), BF16), pltpu.VMEM((d_mix, d), BF16),
                        pltpu.VMEM(_stage_shape(w_in), F32), pltpu.VMEM(_stage_shape(w_out), F32),
                        pltpu.SemaphoreType.DMA,
                        pltpu.VMEM((ts, d_in), F32), pltpu.VMEM((ts, d_in), F32),
                        pltpu.VMEM((HALO + ts, d_mix), F32),
                        pltpu.VMEM((ts, d_mix), BF16), pltpu.VMEM((ts, d_mix), BF16)],
        compiler_params=_params(),
        name="mixer",
    )(x, x, g, w_in, *small, w_out)


def _attn_body(layer, x_ref, g_ref, wq_hbm, k_ref, v_ref, wo_hbm, wq_ref, wo_ref, stage_ref, sem):
    @pl.when(pl.program_id(0) == 0)
    def _():
        _load_cast(wq_hbm, layer, wq_ref, stage_ref, sem)
        _load_cast(wo_hbm, layer, wo_ref, stage_ref, sem)

    d = x_ref.shape[1]
    dh = d // N_MEM_HEADS
    x = x_ref[...]
    h = _rms(x, g_ref[...]).astype(BF16)
    q = _dot(h, wq_ref[...])
    outs = []
    for hh in range(N_MEM_HEADS):
        sl = slice(hh * dh, (hh + 1) * dh)
        s = lax.dot_general(q[:, sl].astype(BF16), k_ref[:, sl], (((1,), (1,)), ((), ())),
                            preferred_element_type=F32) * (dh ** -0.5)
        e = jnp.exp(s - jnp.max(s, axis=-1, keepdims=True))
        p = e * (1.0 / jnp.sum(e, axis=-1, keepdims=True))
        outs.append(_dot(p.astype(BF16), v_ref[:, sl]).astype(BF16))
    return x + _dot(jnp.concatenate(outs, axis=-1), wo_ref[...])


def _attn_kernel(layer, x_ref, g_ref, wq_hbm, k_ref, v_ref, wo_hbm, o_ref, wq_ref, wo_ref, stage_ref, sem):
    o_ref[...] = _attn_body(layer, x_ref, g_ref, wq_hbm, k_ref, v_ref, wo_hbm, wq_ref, wo_ref, stage_ref, sem)


def _attn_route_kernel(layer, x_ref, g_ref, wq_hbm, k_ref, v_ref, wo_hbm, fg_ref, wr_ref, tri_ref,
                       o_ref, h2_ref, ri_ref, rg_ref, cnt_ref, wq_ref, wo_ref, stage_ref, sem, carry_ref):
    i = pl.program_id(0)

    @pl.when(i == 0)
    def _():
        carry_ref[...] = jnp.zeros(carry_ref.shape, F32)

    xn = _attn_body(layer, x_ref, g_ref, wq_hbm, k_ref, v_ref, wo_hbm, wq_ref, wo_ref, stage_ref, sem)
    o_ref[...] = xn
    h2 = _rms(xn, fg_ref[...])
    h2_ref[...] = h2

    h_hi = h2.astype(BF16)
    h_lo = (h2 - h_hi.astype(F32)).astype(BF16)
    part = _dot(h_hi, wr_ref[...])
    logits = part[:, 0:LANES] + part[:, LANES:2 * LANES] + _dot(h_lo, wr_ref[:, 0:LANES])

    ts = x_ref.shape[0]
    lane = lax.broadcasted_iota(jnp.int32, (ts, LANES), 1).astype(F32)
    neg = jnp.float32(-jnp.inf)
    lg = jnp.where(lane < N_EXPERTS, logits, neg)
    v1 = jnp.max(lg, axis=-1, keepdims=True)
    i1 = jnp.min(jnp.where(lg == v1, lane, float(LANES)), axis=-1, keepdims=True)
    lg2 = jnp.where(lane == i1, neg, lg)
    v2 = jnp.max(lg2, axis=-1, keepdims=True)
    i2 = jnp.min(jnp.where(lg2 == v2, lane, float(LANES)), axis=-1, keepdims=True)
    e2 = jnp.exp(v2 - v1)
    den = 1.0 + e2
    w1 = 1.0 / den
    w2 = e2 / den

    sel1 = lane == i1
    sel2 = lane == i2
    onehot = jnp.where(sel1 | sel2, 1.0, 0.0)
    ranks = carry_ref[0:1, :] + _dot(tri_ref[...], onehot.astype(BF16))
    r1 = jnp.sum(jnp.where(sel1, ranks, 0.0), axis=-1, keepdims=True)
    r2 = jnp.sum(jnp.where(sel2, ranks, 0.0), axis=-1, keepdims=True)
    total = carry_ref[0:1, :] + jnp.sum(onehot, axis=0, keepdims=True)
    carry_ref[...] = jnp.broadcast_to(total, carry_ref.shape)
    cnt_ref[...] = jnp.broadcast_to(total, cnt_ref.shape)

    meta = jnp.where(lane == 0, i1, jnp.where(lane == 1, i2, jnp.where(lane == 2, r1, jnp.where(lane == 3, r2, 0.0))))
    ri_ref[...] = meta.astype(jnp.int32)
    rg_ref[...] = jnp.where(lane == 0, w1, jnp.where(lane == 1, w2, 0.0))


def _attn(x, layer, seq_len, m_len, g, w_q, kv, w_o, route=None):
    n, d = x.shape
    ts = min(TS_ATT, seq_len)
    tps = seq_len // ts
    x_spec = pl.BlockSpec((ts, d), lambda i: (i, 0))
    in_specs = [x_spec, _const_spec((1, d)), _ANY,
                pl.BlockSpec((m_len, d), lambda i: (i // tps, 0)),
                pl.BlockSpec((m_len, d), lambda i: (i // tps, 1)),
                _ANY]
    args = [x, g, w_q, kv, kv, w_o]
    w_scratch = [pltpu.VMEM((d, d), BF16), pltpu.VMEM((d, d), BF16),
                 pltpu.VMEM(_stage_shape(w_q), F32), pltpu.SemaphoreType.DMA]
    if route is None:
        return pl.pallas_call(
            functools.partial(_attn_kernel, layer), grid=(n // ts,), in_specs=in_specs, out_specs=x_spec,
            out_shape=jax.ShapeDtypeStruct((n, d), F32), scratch_shapes=w_scratch,
            compiler_params=_params(), name="attn",
        )(*args)
    ffn_g, w_r2 = route
    tri = jnp.tril(jnp.ones((ts, ts), BF16), -1)
    lane_spec = pl.BlockSpec((ts, LANES), lambda i: (i, 0))
    return pl.pallas_call(
        functools.partial(_attn_route_kernel, layer), grid=(n // ts,),
        in_specs=in_specs + [_const_spec((1, d)), _const_spec(w_r2.shape), _const_spec(tri.shape)],
        out_specs=[x_spec, x_spec, lane_spec, lane_spec,
                   pl.BlockSpec((SUBLANES, LANES), lambda i: (0, 0))],
        out_shape=[jax.ShapeDtypeStruct((n, d), F32), jax.ShapeDtypeStruct((n, d), F32),
                   jax.ShapeDtypeStruct((n, LANES), jnp.int32), jax.ShapeDtypeStruct((n, LANES), F32),
                   jax.ShapeDtypeStruct((SUBLANES, LANES), F32)],
        scratch_shapes=w_scratch + [pltpu.VMEM((SUBLANES, LANES), F32)],
        compiler_params=_params(), name="attn_route",
    )(*args, ffn_g, w_r2, tri)


def _ffn_kernel(layer, chunks, d_ff, x_ref, g_ref, wgu_hbm, wd_hbm, o_ref,
                wgu_ref, wd_ref, stage_gu, stage_d, sem):
    @pl.when(pl.program_id(0) == 0)
    def _():
        _load_cast(wgu_hbm, layer, wgu_ref, stage_gu, sem)
        _load_cast(wd_hbm, layer, wd_ref, stage_d, sem)

    x = x_ref[...]
    h = _rms(x, g_ref[...]).astype(BF16)
    acc = x
    for lo, hi in chunks:
        gate = _dot(h, wgu_ref[:, lo:hi])
        up = _dot(h, wgu_ref[:, d_ff + lo:d_ff + hi])
        a = (gate * _sigmoid(gate) * up).astype(BF16)
        acc = acc + _dot(a, wd_ref[lo:hi, :])
    o_ref[...] = acc


def _ffn(x, layer, g, w_gu, w_down):
    n, d = x.shape
    ts = min(TS_FFN, n)
    d_ff = w_down.shape[1]
    step = 1024 if d_ff > 1024 else d_ff
    chunks = tuple((lo, min(lo + step, d_ff)) for lo in range(0, d_ff, step))
    x_spec = pl.BlockSpec((ts, d), lambda i: (i, 0))
    return pl.pallas_call(
        functools.partial(_ffn_kernel, layer, chunks, d_ff), grid=(n // ts,),
        in_specs=[x_spec, _const_spec((1, d)), _ANY, _ANY],
        out_specs=x_spec, out_shape=jax.ShapeDtypeStruct((n, d), F32),
        scratch_shapes=[pltpu.VMEM(w_gu.shape[1:], BF16), pltpu.VMEM(w_down.shape[1:], BF16),
                        pltpu.VMEM(_stage_shape(w_gu), F32), pltpu.VMEM(_stage_shape(w_down), F32),
                        pltpu.SemaphoreType.DMA],
        compiler_params=_params(), name="ffn_dense",
    )(x, g, w_gu, w_down)


def _row_copy(src, src_row, dst, dst_row, sem):
    return pltpu.make_async_copy(src.at[pl.ds(src_row, 1)], dst.at[pl.ds(dst_row, 1)], sem)


def _dispatch_kernel(td, tm, starts_ref, ntiles_ref, pos_ref, h_ref, xs_hbm, zero_ref, zsem, sem):
    i = pl.program_id(0)
    zr = zero_ref.shape[0]

    @pl.when(i == 0)
    def _():
        zero_ref[...] = jnp.zeros(zero_ref.shape, F32)

        def zero_copies(e):
            base = pl.multiple_of(starts_ref[e] + (ntiles_ref[e] - 1) * tm, tm)
            return [pltpu.make_async_copy(zero_ref, xs_hbm.at[pl.ds(base + q * zr, zr)], zsem)
                    for q in range(tm // zr)]

        for e in range(N_EXPERTS):
            @pl.when(ntiles_ref[e] > 0)
            def _():
                for cp in zero_copies(e):
                    cp.start()
        for e in range(N_EXPERTS):
            @pl.when(ntiles_ref[e] > 0)
            def _():
                for cp in zero_copies(e):
                    cp.wait()

        n_used = ntiles_ref[0]
        for e in range(1, N_EXPERTS):
            n_used = n_used + ntiles_ref[e]

        def clear_tile(j, carry):
            base = pl.multiple_of(j * tm, tm)
            copies = [pltpu.make_async_copy(zero_ref, xs_hbm.at[pl.ds(base + q * zr, zr)], zsem)
                      for q in range(tm // zr)]
            for cp in copies:
                cp.start()
            for cp in copies:
                cp.wait()
            return carry

        lax.fori_loop(n_used, xs_hbm.shape[0] // tm, clear_tile, 0)

    def issue(t, carry):
        for k in range(2):
            _row_copy(h_ref, t, xs_hbm, pos_ref[0, 0, 2 * t + k], sem).start(priority=k)
        return carry

    def drain(t, carry):
        for k in range(2):
            _row_copy(h_ref, 0, xs_hbm, 0, sem).wait()
        return carry

    lax.fori_loop(0, td, issue, 0, unroll=DMA_UNROLL)
    lax.fori_loop(0, td, drain, 0, unroll=DMA_UNROLL)


def _dispatch(h2, pos, starts, ntiles, r_pad, td, tm):
    n, d = h2.shape
    zr = min(ZR_MOE, tm)
    grid_spec = pltpu.PrefetchScalarGridSpec(
        num_scalar_prefetch=2, grid=(n // td,),
        in_specs=[pl.BlockSpec((1, 1, 2 * td), lambda i, *_: (i, 0, 0), memory_space=pltpu.SMEM),
                  pl.BlockSpec((td, d), lambda i, *_: (i, 0))],
        out_specs=_ANY,
        scratch_shapes=[pltpu.VMEM((zr, d), F32), pltpu.SemaphoreType.DMA, pltpu.SemaphoreType.DMA])
    return pl.pallas_call(
        functools.partial(_dispatch_kernel, td, tm), grid_spec=grid_spec,
        out_shape=jax.ShapeDtypeStruct((r_pad, d), F32),
        compiler_params=_params(), name="moe_dispatch",
    )(starts, ntiles, pos, h2)


def _gmm_kernel(te_ref, nu_ref, xs_ref, wg_ref, wu_ref, wd_ref, o_ref, xb_ref):
    i = pl.program_id(0)
    c = pl.program_id(1)

    @pl.when(i < nu_ref[0])
    def _():
        @pl.when(c == 0)
        def _():
            xb_ref[...] = xs_ref[...].astype(BF16)
            o_ref[...] = jnp.zeros(o_ref.shape, F32)

        x = xb_ref[...]
        gate = _dot(x, wg_ref[...].astype(BF16))
        up = _dot(x, wu_ref[...].astype(BF16))
        a = (gate * _sigmoid(gate) * up).astype(BF16)
        o_ref[...] += _dot(a, wd_ref[...].astype(BF16))

    @pl.when(jnp.logical_and(i >= nu_ref[0], c == 0))
    def _():
        o_ref[...] = jnp.zeros(o_ref.shape, F32)


def _gmm(xs, layer, w_gu, w_down, tile_expert, n_used, tm, fc):
    r_pad, d = xs.shape
    d_ff = w_down.shape[2]
    n_chunks = d_ff // fc
    n_tiles = r_pad // tm

    def tile(i, nu):
        return jnp.minimum(i, nu[0] - 1)

    def chunk(i, c, nu):
        return jnp.where(i < nu[0], c, n_chunks - 1)

    grid_spec = pltpu.PrefetchScalarGridSpec(
        num_scalar_prefetch=2, grid=(n_tiles, n_chunks),
        in_specs=[pl.BlockSpec((tm, d), lambda i, c, te, nu: (tile(i, nu), 0)),
                  pl.BlockSpec((None, None, d, fc),
                               lambda i, c, te, nu: (layer, te[tile(i, nu)], 0, chunk(i, c, nu))),
                  pl.BlockSpec((None, None, d, fc),
                               lambda i, c, te, nu: (layer, te[tile(i, nu)], 0, n_chunks + chunk(i, c, nu))),
                  pl.BlockSpec((None, None, fc, d),
                               lambda i, c, te, nu: (layer, te[tile(i, nu)], chunk(i, c, nu), 0))],
        out_specs=pl.BlockSpec((tm, d), lambda i, c, te, nu: (i, 0)),
        scratch_shapes=[pltpu.VMEM((tm, d), BF16)])
    return pl.pallas_call(
        _gmm_kernel, grid_spec=grid_spec, out_shape=jax.ShapeDtypeStruct((r_pad, d), F32),
        compiler_params=_params(2), name="moe_gmm",
    )(tile_expert, n_used, xs, w_gu, w_gu, w_down)


def _combine_kernel(td, final, pos_ref, x_ref, rg_ref, fg_ref, ys_hbm, o_ref, ybuf, sem):
    def issue(t, carry):
        for k in range(2):
            _row_copy(ys_hbm, pos_ref[0, 0, 2 * t + k], ybuf.at[k], t, sem).start(priority=k)
        return carry

    def drain(t, carry):
        for k in range(2):
            _row_copy(ys_hbm, 0, ybuf.at[k], 0, sem).wait()
        return carry

    lax.fori_loop(0, td, issue, 0, unroll=DMA_UNROLL)
    lax.fori_loop(0, td, drain, 0, unroll=DMA_UNROLL)
    out = x_ref[...] + rg_ref[:, 0:1] * ybuf[0] + rg_ref[:, 1:2] * ybuf[1]
    if final:
        out = _rms(out, fg_ref[...])
    o_ref[...] = out


def _combine(x, rg, pos, ys, final_g, final, td):
    n, d = x.shape
    x_spec = pl.BlockSpec((td, d), lambda i: (i, 0))
    return pl.pallas_call(
        functools.partial(_combine_kernel, td, final), grid=(n // td,),
        in_specs=[pl.BlockSpec((1, 1, 2 * td), lambda i: (i, 0, 0), memory_space=pltpu.SMEM),
                  x_spec,
                  pl.BlockSpec((td, LANES), lambda i: (i, 0)),
                  pl.BlockSpec((1, d), lambda i: (0, 0)),
                  _ANY],
        out_specs=x_spec,
        out_shape=jax.ShapeDtypeStruct((n, d), F32),
        scratch_shapes=[pltpu.VMEM((2, td, d), F32), pltpu.SemaphoreType.DMA],
        compiler_params=_params(), name="moe_combine",
    )(pos, x, rg, final_g, ys)


def _moe(x, layer, h2, ri, rg, counts, w_gu, w_down, final_g, final):
    n, d = x.shape
    tm = min(TM_MOE, n)
    td = min(TD_MOE, n)
    fc = min(FC_MOE, w_down.shape[2])
    max_tiles = (2 * n) // tm + N_EXPERTS
    cnt = counts[0, :N_EXPERTS].astype(jnp.int32)
    ntiles = (cnt + tm - 1) // tm
    cum = jnp.cumsum(ntiles)
    starts = (cum - ntiles) * tm
    n_used = cum[-1:]
    tile_ids = jnp.arange(max_tiles, dtype=jnp.int32)
    tile_expert = jnp.minimum(jnp.sum((tile_ids[:, None] >= cum[None, :]).astype(jnp.int32), axis=1),
                              N_EXPERTS - 1)
    start_of = jnp.sum(jnp.where(ri[:, 0:2, None] == jnp.arange(N_EXPERTS, dtype=jnp.int32), starts, 0), axis=-1)
    pos = (start_of + ri[:, 2:4]).reshape(n // td, 1, 2 * td)
    xs = _dispatch(h2, pos, starts, ntiles, max_tiles * tm, td, tm)
    ys = _gmm(xs, layer, w_gu, w_down, tile_expert, n_used, tm, fc)
    return _combine(x, rg, pos, ys, final_g, final, td)


def _row(v):
    return v.reshape(1, -1).astype(F32)


def _block_diag(pool_w):
    g, c, _ = pool_w.shape
    out = jnp.zeros((g * c, g * c), pool_w.dtype)
    for j in range(g):
        out = lax.dynamic_update_slice(out, pool_w[j], (j * c, j * c))
    return out


def _router_hi_lo(w_router):
    d, e = w_router.shape
    hi = w_router.astype(BF16)
    lo = (w_router - hi.astype(F32)).astype(BF16)
    pad = jnp.zeros((d, LANES - e), BF16)
    return jnp.concatenate([hi, pad, lo, pad], axis=1)


def kernel(x, mem, mix_norm_g, w_in, pool_w, pool_scale, sconv_w, conf_w, conf_b, conf_ln_g, conf_ln_b, group_norm_g, w_out, mem_norm_g, w_mem_kv, xattn_norm_g, w_q, w_o, ffn_norm_g, w_gu_dense, w_down_dense, w_router, w_gu_moe, w_down_moe, final_norm_g):
    b, s, d = x.shape
    m_len = mem.shape[1]
    depth = w_in.shape[0]
    assert depth % 2 == 0, "the last layer must be a routed layer (it applies the final norm)"
    kv = _memkv(mem.reshape(b * m_len, d), _row(mem_norm_g), w_mem_kv, m_len)
    xf = x.reshape(b * s, d)
    for l in range(depth):
        xf = _mixer(xf, l, s, _row(mix_norm_g[l]), w_in, _block_diag(pool_w[l]).astype(BF16),
                    _row(pool_scale[l]), sconv_w[l], conf_w[l], _row(conf_b[l]), _row(conf_ln_g[l]),
                    _row(conf_ln_b[l]), _row(group_norm_g[l]), w_out)
        if l % 2 == 0:
            xf = _attn(xf, l, s, m_len, _row(xattn_norm_g[l]), w_q, kv, w_o)
            xf = _ffn(xf, l // 2, _row(ffn_norm_g[l]), w_gu_dense, w_down_dense)
        else:
            xf, h2, ri, rg, counts = _attn(
                xf, l, s, m_len, _row(xattn_norm_g[l]), w_q, kv, w_o,
                route=(_row(ffn_norm_g[l]), _router_hi_lo(w_router[l // 2])))
            xf = _moe(xf, l // 2, h2, ri, rg, counts, w_gu_moe, w_down_moe, _row(final_norm_g), l == depth - 1)
    return xf.reshape(b, s, d)
```

```python
import functools

import jax
import jax.numpy as jnp
from jax import lax
from jax.experimental import pallas as pl
from jax.experimental.pallas import tpu as pltpu

F32 = jnp.float32
BF16 = jnp.bfloat16
EPS = 1e-6

LANES = 128
SUBLANES = 8
VMEM_LIMIT_BYTES = 56 * 1024 * 1024

POOL_WINDOWS = (2, 4, 8, 16)
N_MEM_HEADS = 4
N_EXPERTS = 8
HALO = 32

TS_MIX = 512
RC_MIX = 64
TS_ATT = 1024
TS_FFN = 512
TM_MOE = 1024
FC_MOE = 512
GMM_ROW_PARTS = 4
TD_MOE = 1024
ZR_MOE = 256
DMA_UNROLL = 8
STAGE_BYTES = 3 * 1024 * 1024


def _params(n_axes=1):
    return pltpu.CompilerParams(dimension_semantics=("arbitrary",) * n_axes,
                                vmem_limit_bytes=VMEM_LIMIT_BYTES)


def _const_spec(shape):
    nd = len(shape)
    return pl.BlockSpec(shape, lambda *_: (0,) * nd, pipeline_mode=pl.Buffered(1))


_ANY = pl.BlockSpec(memory_space=pl.ANY)


def _rms(x, g):
    ms = jnp.mean(x * x, axis=-1, keepdims=True)
    return x * lax.rsqrt(ms + EPS) * g


def _dot(a, b):
    return jnp.dot(a, b, preferred_element_type=F32)


def _sigmoid(x):
    return 1.0 / (1.0 + jnp.exp(-x))


def _stage_rows(k, n):
    rows = 16
    while k % (2 * rows) == 0 and 2 * rows * n * 4 <= STAGE_BYTES:
        rows *= 2
    assert k % rows == 0
    return rows


def _stage_shape(w):
    k, n = w.shape[-2:]
    return (_stage_rows(k, n), n)


def _load_cast(w_hbm, layer, w_ref, stage_ref, sem):
    rows = stage_ref.shape[0]

    def body(j, carry):
        r = pl.multiple_of(j * rows, rows)
        cp = pltpu.make_async_copy(w_hbm.at[layer, pl.ds(r, rows)], stage_ref, sem)
        cp.start()
        cp.wait()
        w_ref[pl.ds(r, rows), :] = stage_ref[...].astype(BF16)
        return carry

    lax.fori_loop(0, w_ref.shape[0] // rows, body, 0)


def _memkv_kernel(mem_ref, g_ref, w_ref, kv_ref):
    h = _rms(mem_ref[...], g_ref[...]).astype(BF16)
    kv_ref[...] = _dot(h, w_ref[...].astype(BF16)).astype(BF16)


def _memkv(mem2d, g, w_kv, m_len):
    n, d = mem2d.shape
    return pl.pallas_call(
        _memkv_kernel,
        grid=(n // m_len,),
        in_specs=[pl.BlockSpec((m_len, d), lambda i: (i, 0)),
                  _const_spec((1, d)),
                  _const_spec(w_kv.shape)],
        out_specs=pl.BlockSpec((m_len, w_kv.shape[1]), lambda i: (i, 0)),
        out_shape=jax.ShapeDtypeStruct((n, w_kv.shape[1]), BF16),
        compiler_params=_params(),
        name="mem_kv",
    )(mem2d, g, w_kv)


def _mixer_kernel(layer, tiles_per_seq, ts, rc, d_pool, d_sc,
                  xa_ref, xc_ref, g_ref, win_hbm, wbd_ref, pscale_ref, sconv_ref, confw_ref, confb_ref,
                  lng_ref, lnb_ref, gng_ref, wout_hbm, o_ref,
                  win_ref, wout_ref, stage_in, stage_out, sem, z0_ref, z1_ref, ext_ref, y0_ref, y1_ref,
                  pb_ref):
    c0, c1, c2, c3, c4 = d_pool, d_pool + d_sc, d_pool + 2 * d_sc, d_pool + 3 * d_sc, d_pool + 4 * d_sc
    c5 = d_pool + 5 * d_sc
    d_mix = d_pool + 2 * d_sc
    n_taps = confw_ref.shape[0]
    grp = d_pool // len(POOL_WINDOWS)
    j = pl.program_id(0)

    @pl.when(j == 0)
    def _():
        _load_cast(win_hbm, layer, win_ref, stage_in, sem)
        _load_cast(wout_hbm, layer, wout_ref, stage_out, sem)
        z1_ref[...] = jnp.zeros(z1_ref.shape, F32)
        y0_ref[...] = jnp.zeros(y0_ref.shape, BF16)
        ext_ref[...] = jnp.zeros(ext_ref.shape, F32)

    lane = lax.broadcasted_iota(jnp.int32, (rc, d_pool), 1)
    row = lax.broadcasted_iota(jnp.int32, (rc, d_pool), 0)
    win = jnp.where(lane < grp, POOL_WINDOWS[0],
                    jnp.where(lane < 2 * grp, POOL_WINDOWS[1],
                              jnp.where(lane < 3 * grp, POOL_WINDOWS[2], POOL_WINDOWS[3])))

    def stage_a(rows, z_ref):
        h = _rms(xa_ref[rows, :], g_ref[...]).astype(BF16)
        z_ref[...] = _dot(h, win_ref[...])

    def stage_c(rows, y_ref):
        o_ref[rows, :] = xc_ref[rows, :] + _dot(y_ref[...], wout_ref[...])

    def chunk(c, tile_in_seq, z_ref, y_ref):
        r0 = c * rc

        def zc(lo, hi):
            return z_ref[pl.ds(r0, rc), lo:hi]

        ext_ref[pl.ds(HALO + r0, rc), 0:c0] = zc(0, c0)
        ext_ref[pl.ds(HALO + r0, rc), c0:c1] = zc(c2, c3) * zc(c0, c1)
        ext_ref[pl.ds(HALO + r0, rc), c1:d_mix] = zc(c3, c4) * _sigmoid(zc(c4, c5))

        e = ext_ref[pl.ds(r0 + 8, rc + 24), 0:c0]
        s2 = e + pltpu.roll(e, 1, 0)
        s4 = s2 + pltpu.roll(s2, 2, 0)
        s8 = s4 + pltpu.roll(s4, 4, 0)
        s16 = s8[24:] + s8[16:16 + rc]
        ssum = jnp.where(lane < grp, s2[24:],
                         jnp.where(lane < 2 * grp, s4[24:],
                                   jnp.where(lane < 3 * grp, s8[24:], s16)))
        tpos = tile_in_seq * ts + r0 + row
        cnt = jnp.minimum(tpos + 1, win).astype(F32)
        pooled = ssum / cnt - e[24:]
        ya = _dot(pooled.astype(BF16), wbd_ref[...]) * pscale_ref[...]
        y_ref[pl.ds(r0, rc), 0:c0] = _rms(ya, gng_ref[:, 0:c0]).astype(BF16)

        sx = ext_ref[pl.ds(r0 + 24, rc + 8), c0:c1]
        conv = (sx[8:] * sconv_ref[2:3, :]
                + pltpu.roll(sx, 1, 0)[8:] * sconv_ref[1:2, :]
                + pltpu.roll(sx, 2, 0)[8:] * sconv_ref[0:1, :])
        yb = zc(c1, c2) * conv
        y_ref[pl.ds(r0, rc), c0:c1] = _rms(yb, gng_ref[:, c0:c1]).astype(BF16)

        hcs = []
        for wl in range(0, d_sc, LANES):
            wh = wl + LANES
            acc = None
            for b in range(SUBLANES):
                pb = None
                for a in range((n_taps + SUBLANES - 1) // SUBLANES):
                    d = SUBLANES * a + b
                    if d >= n_taps:
                        continue
                    xa = ext_ref[pl.ds(r0 + 24 - SUBLANES * a, rc + 8), c1 + wl:c1 + wh]
                    k = n_taps - 1 - d
                    term = xa * confw_ref[k:k + 1, wl:wh]
                    pb = term if pb is None else pb + term
                if b:
                    pb_ref[b, :, :] = pb
                    shifted = pb_ref[b, 8 - b:8 - b + rc, :]
                else:
                    shifted = pb[8:]
                acc = shifted if acc is None else acc + shifted
            hcs.append(acc + confb_ref[:, wl:wh])
        hc = jnp.concatenate(hcs, axis=-1)
        mu = jnp.mean(hc, axis=-1, keepdims=True)
        xc = hc - mu
        var = jnp.mean(xc * xc, axis=-1, keepdims=True)
        ln = xc * lax.rsqrt(var + EPS) * lng_ref[...] + lnb_ref[...]
        yc = ln * _sigmoid(ln)
        y_ref[pl.ds(r0, rc), c1:d_mix] = _rms(yc, gng_ref[:, c1:d_mix]).astype(BF16)

    def stage_b(tile, z_ref, y_ref):
        tile_in_seq = (tile + tiles_per_seq) % tiles_per_seq
        ext_ref[0:HALO, :] = jnp.where(tile_in_seq == 0, 0.0, ext_ref[0:HALO, :])
        for c in range(ts // rc):
            chunk(c, tile_in_seq, z_ref, y_ref)
        ext_ref[0:HALO, :] = ext_ref[ts:ts + HALO, :]

    lo_rows, hi_rows = slice(0, ts), slice(ts, 2 * ts)
    stage_c(lo_rows, y0_ref)
    stage_a(lo_rows, z0_ref)
    stage_b(2 * j - 1, z1_ref, y1_ref)
    stage_a(hi_rows, z1_ref)
    stage_b(2 * j, z0_ref, y0_ref)
    stage_c(hi_rows, y1_ref)


def _mixer(x, layer, seq_len, g, w_in, w_bd, pscale, sconv_w, conf_w, conf_b, ln_g, ln_b, gn_g, w_out):
    n, d = x.shape
    ts = min(TS_MIX, seq_len)
    rc = min(RC_MIX, ts)
    n_pairs = n // (2 * ts)
    d_pool = w_bd.shape[0]
    d_sc = sconv_w.shape[1]
    d_in = w_in.shape[2]
    d_mix = w_out.shape[1]
    assert conf_w.shape[0] <= HALO - 1 and seq_len % ts == 0 and ts % rc == 0 and n % (2 * ts) == 0
    kern = functools.partial(_mixer_kernel, layer, seq_len // ts, ts, rc, d_pool, d_sc)
    small = (w_bd, pscale, sconv_w, conf_w, conf_b, ln_g, ln_b, gn_g)
    last = n_pairs - 1
    return pl.pallas_call(
        kern,
        grid=(n_pairs + 1,),
        in_specs=([pl.BlockSpec((2 * ts, d), lambda j: (jnp.minimum(j, last), 0)),
                   pl.BlockSpec((2 * ts, d), lambda j: (jnp.maximum(j - 1, 0), 0)),
                   _const_spec(g.shape), _ANY]
                  + [_const_spec(c.shape) for c in small] + [_ANY]),
        out_specs=pl.BlockSpec((2 * ts, d), lambda j: (jnp.maximum(j - 1, 0), 0)),
        out_shape=jax.ShapeDtypeStruct((n, d), F32),
        scratch_shapes=[pltpu.VMEM((d, d_in), BF16), pltpu.VMEM((d_mix, d), BF16),
                        pltpu.VMEM(_stage_shape(w_in), F32), pltpu.VMEM(_stage_shape(w_out), F32),
                        pltpu.SemaphoreType.DMA,
                        pltpu.VMEM((ts, d_in), F32), pltpu.VMEM((ts, d_in), F32),
                        pltpu.VMEM((HALO + ts, d_mix), F32),
                        pltpu.VMEM((ts, d_mix), BF16), pltpu.VMEM((ts, d_mix), BF16),
                        pltpu.VMEM((SUBLANES, rc + SUBLANES, LANES), F32)],
        compiler_params=_params(),
        name="mixer",
    )(x, x, g, w_in, *small, w_out)


def _attn_body(layer, x_ref, g_ref, wq_hbm, k_ref, v_ref, wo_hbm, wq_ref, wo_ref, stage_ref, sem):
    @pl.when(pl.program_id(0) == 0)
    def _():
        _load_cast(wq_hbm, layer, wq_ref, stage_ref, sem)
        _load_cast(wo_hbm, layer, wo_ref, stage_ref, sem)

    d = x_ref.shape[1]
    dh = d // N_MEM_HEADS
    x = x_ref[...]
    h = _rms(x, g_ref[...]).astype(BF16)
    q = _dot(h, wq_ref[...])
    outs = []
    for hh in range(N_MEM_HEADS):
        sl = slice(hh * dh, (hh + 1) * dh)
        s = lax.dot_general(q[:, sl].astype(BF16), k_ref[:, sl], (((1,), (1,)), ((), ())),
                            preferred_element_type=F32) * (dh ** -0.5)
        e = jnp.exp(s - jnp.max(s, axis=-1, keepdims=True))
        p = e * (1.0 / jnp.sum(e, axis=-1, keepdims=True))
        outs.append(_dot(p.astype(BF16), v_ref[:, sl]).astype(BF16))
    return x + _dot(jnp.concatenate(outs, axis=-1), wo_ref[...])


def _attn_kernel(layer, x_ref, g_ref, wq_hbm, k_ref, v_ref, wo_hbm, o_ref, wq_ref, wo_ref, stage_ref, sem):
    o_ref[...] = _attn_body(layer, x_ref, g_ref, wq_hbm, k_ref, v_ref, wo_hbm, wq_ref, wo_ref, stage_ref, sem)


def _attn_route_kernel(layer, x_ref, g_ref, wq_hbm, k_ref, v_ref, wo_hbm, fg_ref, wr_ref, tri_ref,
                       o_ref, h2_ref, ri_ref, rg_ref, cnt_ref, wq_ref, wo_ref, stage_ref, sem, carry_ref):
    i = pl.program_id(0)

    @pl.when(i == 0)
    def _():
        carry_ref[...] = jnp.zeros(carry_ref.shape, F32)

    xn = _attn_body(layer, x_ref, g_ref, wq_hbm, k_ref, v_ref, wo_hbm, wq_ref, wo_ref, stage_ref, sem)
    o_ref[...] = xn
    h2 = _rms(xn, fg_ref[...])
    h2_ref[...] = h2

    h_hi = h2.astype(BF16)
    h_lo = (h2 - h_hi.astype(F32)).astype(BF16)
    part = _dot(h_hi, wr_ref[...])
    logits = part[:, 0:LANES] + part[:, LANES:2 * LANES] + _dot(h_lo, wr_ref[:, 0:LANES])

    ts = x_ref.shape[0]
    lane = lax.broadcasted_iota(jnp.int32, (ts, LANES), 1).astype(F32)
    neg = jnp.float32(-jnp.inf)
    lg = jnp.where(lane < N_EXPERTS, logits, neg)
    v1 = jnp.max(lg, axis=-1, keepdims=True)
    i1 = jnp.min(jnp.where(lg == v1, lane, float(LANES)), axis=-1, keepdims=True)
    lg2 = jnp.where(lane == i1, neg, lg)
    v2 = jnp.max(lg2, axis=-1, keepdims=True)
    i2 = jnp.min(jnp.where(lg2 == v2, lane, float(LANES)), axis=-1, keepdims=True)
    e2 = jnp.exp(v2 - v1)
    den = 1.0 + e2
    w1 = 1.0 / den
    w2 = e2 / den

    sel1 = lane == i1
    sel2 = lane == i2
    onehot = jnp.where(sel1 | sel2, 1.0, 0.0)
    ranks = carry_ref[0:1, :] + _dot(tri_ref[...], onehot.astype(BF16))
    r1 = jnp.sum(jnp.where(sel1, ranks, 0.0), axis=-1, keepdims=True)
    r2 = jnp.sum(jnp.where(sel2, ranks, 0.0), axis=-1, keepdims=True)
    total = carry_ref[0:1, :] + jnp.sum(onehot, axis=0, keepdims=True)
    carry_ref[...] = jnp.broadcast_to(total, carry_ref.shape)
    cnt_ref[...] = jnp.broadcast_to(total, cnt_ref.shape)

    meta = jnp.where(lane == 0, i1, jnp.where(lane == 1, i2, jnp.where(lane == 2, r1, jnp.where(lane == 3, r2, 0.0))))
    ri_ref[...] = meta.astype(jnp.int32)
    rg_ref[...] = jnp.where(lane == 0, w1, jnp.where(lane == 1, w2, 0.0))


def _attn(x, layer, seq_len, m_len, g, w_q, kv, w_o, route=None):
    n, d = x.shape
    ts = min(TS_ATT, seq_len)
    tps = seq_len // ts
    x_spec = pl.BlockSpec((ts, d), lambda i: (i, 0))
    in_specs = [x_spec, _const_spec((1, d)), _ANY,
                pl.BlockSpec((m_len, d), lambda i: (i // tps, 0)),
                pl.BlockSpec((m_len, d), lambda i: (i // tps, 1)),
                _ANY]
    args = [x, g, w_q, kv, kv, w_o]
    w_scratch = [pltpu.VMEM((d, d), BF16), pltpu.VMEM((d, d), BF16),
                 pltpu.VMEM(_stage_shape(w_q), F32), pltpu.SemaphoreType.DMA]
    if route is None:
        return pl.pallas_call(
            functools.partial(_attn_kernel, layer), grid=(n // ts,), in_specs=in_specs, out_specs=x_spec,
            out_shape=jax.ShapeDtypeStruct((n, d), F32), scratch_shapes=w_scratch,
            compiler_params=_params(), name="attn",
        )(*args)
    ffn_g, w_r2 = route
    tri = jnp.tril(jnp.ones((ts, ts), BF16), -1)
    lane_spec = pl.BlockSpec((ts, LANES), lambda i: (i, 0))
    return pl.pallas_call(
        functools.partial(_attn_route_kernel, layer), grid=(n // ts,),
        in_specs=in_specs + [_const_spec((1, d)), _const_spec(w_r2.shape), _const_spec(tri.shape)],
        out_specs=[x_spec, x_spec, lane_spec, lane_spec,
                   pl.BlockSpec((SUBLANES, LANES), lambda i: (0, 0))],
        out_shape=[jax.ShapeDtypeStruct((n, d), F32), jax.ShapeDtypeStruct((n, d), F32),
                   jax.ShapeDtypeStruct((n, LANES), jnp.int32), jax.ShapeDtypeStruct((n, LANES), F32),
                   jax.ShapeDtypeStruct((SUBLANES, LANES), F32)],
        scratch_shapes=w_scratch + [pltpu.VMEM((SUBLANES, LANES), F32)],
        compiler_params=_params(), name="attn_route",
    )(*args, ffn_g, w_r2, tri)


def _ffn_kernel(layer, chunks, d_ff, x_ref, g_ref, wgu_hbm, wd_hbm, o_ref,
                wgu_ref, wd_ref, stage_gu, stage_d, sem):
    @pl.when(pl.program_id(0) == 0)
    def _():
        _load_cast(wgu_hbm, layer, wgu_ref, stage_gu, sem)
        _load_cast(wd_hbm, layer, wd_ref, stage_d, sem)

    x = x_ref[...]
    h = _rms(x, g_ref[...]).astype(BF16)
    acc = x
    for lo, hi in chunks:
        gate = _dot(h, wgu_ref[:, lo:hi])
        up = _dot(h, wgu_ref[:, d_ff + lo:d_ff + hi])
        a = (gate * _sigmoid(gate) * up).astype(BF16)
        acc = acc + _dot(a, wd_ref[lo:hi, :])
    o_ref[...] = acc


def _ffn(x, layer, g, w_gu, w_down):
    n, d = x.shape
    ts = min(TS_FFN, n)
    d_ff = w_down.shape[1]
    step = 1024 if d_ff > 1024 else d_ff
    chunks = tuple((lo, min(lo + step, d_ff)) for lo in range(0, d_ff, step))
    x_spec = pl.BlockSpec((ts, d), lambda i: (i, 0))
    return pl.pallas_call(
        functools.partial(_ffn_kernel, layer, chunks, d_ff), grid=(n // ts,),
        in_specs=[x_spec, _const_spec((1, d)), _ANY, _ANY],
        out_specs=x_spec, out_shape=jax.ShapeDtypeStruct((n, d), F32),
        scratch_shapes=[pltpu.VMEM(w_gu.shape[1:], BF16), pltpu.VMEM(w_down.shape[1:], BF16),
                        pltpu.VMEM(_stage_shape(w_gu), F32), pltpu.VMEM(_stage_shape(w_down), F32),
                        pltpu.SemaphoreType.DMA],
        compiler_params=_params(), name="ffn_dense",
    )(x, g, w_gu, w_down)


def _row_copy(src, src_row, dst, dst_row, sem):
    return pltpu.make_async_copy(src.at[pl.ds(src_row, 1)], dst.at[pl.ds(dst_row, 1)], sem)


def _dispatch_kernel(td, tm, starts_ref, ntiles_ref, pos_ref, h_ref, xs_hbm, zero_ref, zsem, sem):
    i = pl.program_id(0)
    zr = zero_ref.shape[0]

    @pl.when(i == 0)
    def _():
        zero_ref[...] = jnp.zeros(zero_ref.shape, F32)

        def zero_copies(e):
            base = pl.multiple_of(starts_ref[e] + (ntiles_ref[e] - 1) * tm, tm)
            return [pltpu.make_async_copy(zero_ref, xs_hbm.at[pl.ds(base + q * zr, zr)], zsem)
                    for q in range(tm // zr)]

        for e in range(N_EXPERTS):
            @pl.when(ntiles_ref[e] > 0)
            def _():
                for cp in zero_copies(e):
                    cp.start()
        for e in range(N_EXPERTS):
            @pl.when(ntiles_ref[e] > 0)
            def _():
                for cp in zero_copies(e):
                    cp.wait()

        n_used = ntiles_ref[0]
        for e in range(1, N_EXPERTS):
            n_used = n_used + ntiles_ref[e]

        def clear_tile(j, carry):
            base = pl.multiple_of(j * tm, tm)
            copies = [pltpu.make_async_copy(zero_ref, xs_hbm.at[pl.ds(base + q * zr, zr)], zsem)
                      for q in range(tm // zr)]
            for cp in copies:
                cp.start()
            for cp in copies:
                cp.wait()
            return carry

        lax.fori_loop(n_used, xs_hbm.shape[0] // tm, clear_tile, 0)

    def issue(t, carry):
        for k in range(2):
            _row_copy(h_ref, t, xs_hbm, pos_ref[0, 0, 2 * t + k], sem).start(priority=k)
        return carry

    def drain(t, carry):
        for k in range(2):
            _row_copy(h_ref, 0, xs_hbm, 0, sem).wait()
        return carry

    lax.fori_loop(0, td, issue, 0, unroll=DMA_UNROLL)
    lax.fori_loop(0, td, drain, 0, unroll=DMA_UNROLL)


def _dispatch(h2, pos, starts, ntiles, r_pad, td, tm):
    n, d = h2.shape
    zr = min(ZR_MOE, tm)
    grid_spec = pltpu.PrefetchScalarGridSpec(
        num_scalar_prefetch=2, grid=(n // td,),
        in_specs=[pl.BlockSpec((1, 1, 2 * td), lambda i, *_: (i, 0, 0), memory_space=pltpu.SMEM),
                  pl.BlockSpec((td, d), lambda i, *_: (i, 0))],
        out_specs=_ANY,
        scratch_shapes=[pltpu.VMEM((zr, d), F32), pltpu.SemaphoreType.DMA, pltpu.SemaphoreType.DMA])
    return pl.pallas_call(
        functools.partial(_dispatch_kernel, td, tm), grid_spec=grid_spec,
        out_shape=jax.ShapeDtypeStruct((r_pad, d), F32),
        compiler_params=_params(), name="moe_dispatch",
    )(starts, ntiles, pos, h2)


def _gmm_kernel(te_ref, nu_ref, valid_ref, xs_ref, wg_ref, wu_ref, wd_ref, o_ref, xb_ref):
    i = pl.program_id(0)
    c = pl.program_id(1)
    tm = xs_ref.shape[0]
    quarter = tm // GMM_ROW_PARTS

    @pl.when(i < nu_ref[0])
    def _():
        @pl.when(c == 0)
        def _():
            xb_ref[...] = xs_ref[...].astype(BF16)
            o_ref[...] = jnp.zeros(o_ref.shape, F32)

        nv = valid_ref[i]
        for part in range(1, GMM_ROW_PARTS + 1):
            rows = part * quarter

            @pl.when(jnp.logical_and(nv > rows - quarter, nv <= rows))
            def _():
                x = xb_ref[0:rows, :]
                gate = _dot(x, wg_ref[...].astype(BF16))
                up = _dot(x, wu_ref[...].astype(BF16))
                a = (gate * _sigmoid(gate) * up).astype(BF16)
                o_ref[0:rows, :] += _dot(a, wd_ref[...].astype(BF16))

    @pl.when(jnp.logical_and(i >= nu_ref[0], c == 0))
    def _():
        o_ref[...] = jnp.zeros(o_ref.shape, F32)


def _gmm(xs, layer, w_gu, w_down, tile_expert, n_used, tile_valid, tm, fc):
    r_pad, d = xs.shape
    d_ff = w_down.shape[2]
    n_chunks = d_ff // fc
    n_tiles = r_pad // tm
    assert tm % (GMM_ROW_PARTS * 2 * SUBLANES) == 0

    def tile(i, nu):
        return jnp.minimum(i, nu[0] - 1)

    def chunk(i, c, nu):
        return jnp.where(i < nu[0], c, n_chunks - 1)

    grid_spec = pltpu.PrefetchScalarGridSpec(
        num_scalar_prefetch=3, grid=(n_tiles, n_chunks),
        in_specs=[pl.BlockSpec((tm, d), lambda i, c, te, nu, nv: (tile(i, nu), 0)),
                  pl.BlockSpec((None, None, d, fc),
                               lambda i, c, te, nu, nv: (layer, te[tile(i, nu)], 0, chunk(i, c, nu))),
                  pl.BlockSpec((None, None, d, fc),
                               lambda i, c, te, nu, nv: (layer, te[tile(i, nu)], 0, n_chunks + chunk(i, c, nu))),
                  pl.BlockSpec((None, None, fc, d),
                               lambda i, c, te, nu, nv: (layer, te[tile(i, nu)], chunk(i, c, nu), 0))],
        out_specs=pl.BlockSpec((tm, d), lambda i, c, te, nu, nv: (i, 0)),
        scratch_shapes=[pltpu.VMEM((tm, d), BF16)])
    return pl.pallas_call(
        _gmm_kernel, grid_spec=grid_spec, out_shape=jax.ShapeDtypeStruct((r_pad, d), F32),
        compiler_params=_params(2), name="moe_gmm",
    )(tile_expert, n_used, tile_valid, xs, w_gu, w_gu, w_down)


def _combine_kernel(td, final, pos_ref, x_ref, rg_ref, fg_ref, ys_hbm, o_ref, ybuf, sem):
    def issue(t, carry):
        for k in range(2):
            _row_copy(ys_hbm, pos_ref[0, 0, 2 * t + k], ybuf.at[k], t, sem).start(priority=k)
        return carry

    def drain(t, carry):
        for k in range(2):
            _row_copy(ys_hbm, 0, ybuf.at[k], 0, sem).wait()
        return carry

    lax.fori_loop(0, td, issue, 0, unroll=DMA_UNROLL)
    lax.fori_loop(0, td, drain, 0, unroll=DMA_UNROLL)
    out = x_ref[...] + rg_ref[:, 0:1] * ybuf[0] + rg_ref[:, 1:2] * ybuf[1]
    if final:
        out = _rms(out, fg_ref[...])
    o_ref[...] = out


def _combine(x, rg, pos, ys, final_g, final, td):
    n, d = x.shape
    x_spec = pl.BlockSpec((td, d), lambda i: (i, 0))
    return pl.pallas_call(
        functools.partial(_combine_kernel, td, final), grid=(n // td,),
        in_specs=[pl.BlockSpec((1, 1, 2 * td), lambda i: (i, 0, 0), memory_space=pltpu.SMEM),
                  x_spec,
                  pl.BlockSpec((td, LANES), lambda i: (i, 0)),
                  pl.BlockSpec((1, d), lambda i: (0, 0)),
                  _ANY],
        out_specs=x_spec,
        out_shape=jax.ShapeDtypeStruct((n, d), F32),
        scratch_shapes=[pltpu.VMEM((2, td, d), F32), pltpu.SemaphoreType.DMA],
        compiler_params=_params(), name="moe_combine",
    )(pos, x, rg, final_g, ys)


def _moe(x, layer, h2, ri, rg, counts, w_gu, w_down, final_g, final):
    n, d = x.shape
    tm = min(TM_MOE, n)
    td = min(TD_MOE, n)
    fc = min(FC_MOE, w_down.shape[2])
    max_tiles = (2 * n) // tm + N_EXPERTS
    cnt = counts[0, :N_EXPERTS].astype(jnp.int32)
    ntiles = (cnt + tm - 1) // tm
    cum = jnp.cumsum(ntiles)
    starts = (cum - ntiles) * tm
    n_used = cum[-1:]
    tile_ids = jnp.arange(max_tiles, dtype=jnp.int32)
    tile_expert = jnp.minimum(jnp.sum((tile_ids[:, None] >= cum[None, :]).astype(jnp.int32), axis=1),
                              N_EXPERTS - 1)
    experts = jnp.arange(N_EXPERTS, dtype=jnp.int32)
    of_tile = tile_expert[:, None] == experts
    rows_before = tile_ids * tm - jnp.sum(jnp.where(of_tile, starts, 0), axis=-1)
    tile_valid = jnp.clip(jnp.sum(jnp.where(of_tile, cnt, 0), axis=-1) - rows_before, 0, tm)
    start_of = jnp.sum(jnp.where(ri[:, 0:2, None] == experts, starts, 0), axis=-1)
    pos = (start_of + ri[:, 2:4]).reshape(n // td, 1, 2 * td)
    xs = _dispatch(h2, pos, starts, ntiles, max_tiles * tm, td, tm)
    ys = _gmm(xs, layer, w_gu, w_down, tile_expert, n_used, tile_valid, tm, fc)
    return _combine(x, rg, pos, ys, final_g, final, td)


def _row(v):
    return v.reshape(1, -1).astype(F32)


def _block_diag(pool_w):
    g, c, _ = pool_w.shape
    out = jnp.zeros((g * c, g * c), pool_w.dtype)
    for j in range(g):
        out = lax.dynamic_update_slice(out, pool_w[j], (j * c, j * c))
    return out


def _router_hi_lo(w_router):
    d, e = w_router.shape
    hi = w_router.astype(BF16)
    lo = (w_router - hi.astype(F32)).astype(BF16)
    pad = jnp.zeros((d, LANES - e), BF16)
    return jnp.concatenate([hi, pad, lo, pad], axis=1)


def kernel(x, mem, mix_norm_g, w_in, pool_w, pool_scale, sconv_w, conf_w, conf_b, conf_ln_g, conf_ln_b, group_norm_g, w_out, mem_norm_g, w_mem_kv, xattn_norm_g, w_q, w_o, ffn_norm_g, w_gu_dense, w_down_dense, w_router, w_gu_moe, w_down_moe, final_norm_g):
    b, s, d = x.shape
    m_len = mem.shape[1]
    depth = w_in.shape[0]
    assert depth % 2 == 0, "the last layer must be a routed layer (it applies the final norm)"
    kv = _memkv(mem.reshape(b * m_len, d), _row(mem_norm_g), w_mem_kv, m_len)
    xf = x.reshape(b * s, d)
    for l in range(depth):
        xf = _mixer(xf, l, s, _row(mix_norm_g[l]), w_in, _block_diag(pool_w[l]).astype(BF16),
                    _row(pool_scale[l]), sconv_w[l], conf_w[l], _row(conf_b[l]), _row(conf_ln_g[l]),
                    _row(conf_ln_b[l]), _row(group_norm_g[l]), w_out)
        if l % 2 == 0:
            xf = _attn(xf, l, s, m_len, _row(xattn_norm_g[l]), w_q, kv, w_o)
            xf = _ffn(xf, l // 2, _row(ffn_norm_g[l]), w_gu_dense, w_down_dense)
        else:
            xf, h2, ri, rg, counts = _attn(
                xf, l, s, m_len, _row(xattn_norm_g[l]), w_q, kv, w_o,
                route=(_row(ffn_norm_g[l]), _router_hi_lo(w_router[l // 2])))
            xf = _moe(xf, l // 2, h2, ri, rg, counts, w_gu_moe, w_down_moe, _row(final_norm_g), l == depth - 1)
    return xf.reshape(b, s, d)
```

```python
import functools

import jax
import jax.numpy as jnp
from jax import lax
from jax.experimental import pallas as pl
from jax.experimental.pallas import tpu as pltpu

F32 = jnp.float32
BF16 = jnp.bfloat16
EPS = 1e-6

LANES = 128
SUBLANES = 8
VMEM_LIMIT_BYTES = 56 * 1024 * 1024

POOL_WINDOWS = (2, 4, 8, 16)
N_MEM_HEADS = 4
N_EXPERTS = 8
HALO = 32

TS_MIX = 512
RC_MIX = 64
TS_ATT = 1024
TS_FFN = 512
TM_MOE = 1024
FC_MOE = 512
GMM_ROW_PARTS = 4
TD_MOE = 1024
ZR_MOE = 256
DMA_UNROLL = 8
STAGE_BYTES = 3 * 1024 * 1024


def _params(n_axes=1):
    return pltpu.CompilerParams(dimension_semantics=("arbitrary",) * n_axes,
                                vmem_limit_bytes=VMEM_LIMIT_BYTES)


def _const_spec(shape):
    nd = len(shape)
    return pl.BlockSpec(shape, lambda *_: (0,) * nd, pipeline_mode=pl.Buffered(1))


_ANY = pl.BlockSpec(memory_space=pl.ANY)


def _rms(x, g):
    ms = jnp.mean(x * x, axis=-1, keepdims=True)
    return x * lax.rsqrt(ms + EPS) * g


def _dot(a, b):
    return jnp.dot(a, b, preferred_element_type=F32)


def _sigmoid(x):
    return 1.0 / (1.0 + jnp.exp(-x))


def _stage_rows(k, n):
    rows = 16
    while k % (2 * rows) == 0 and 2 * rows * n * 4 <= STAGE_BYTES:
        rows *= 2
    assert k % rows == 0
    return rows


def _stage_shape(w):
    k, n = w.shape[-2:]
    return (_stage_rows(k, n), n)


def _load_cast(w_hbm, layer, w_ref, stage_ref, sem):
    rows = stage_ref.shape[0]

    def body(j, carry):
        r = pl.multiple_of(j * rows, rows)
        cp = pltpu.make_async_copy(w_hbm.at[layer, pl.ds(r, rows)], stage_ref, sem)
        cp.start()
        cp.wait()
        w_ref[pl.ds(r, rows), :] = stage_ref[...].astype(BF16)
        return carry

    lax.fori_loop(0, w_ref.shape[0] // rows, body, 0)


def _memkv_kernel(mem_ref, g_ref, w_ref, kv_ref):
    h = _rms(mem_ref[...], g_ref[...]).astype(BF16)
    kv_ref[...] = _dot(h, w_ref[...].astype(BF16)).astype(BF16)


def _memkv(mem2d, g, w_kv, m_len):
    n, d = mem2d.shape
    return pl.pallas_call(
        _memkv_kernel,
        grid=(n // m_len,),
        in_specs=[pl.BlockSpec((m_len, d), lambda i: (i, 0)),
                  _const_spec((1, d)),
                  _const_spec(w_kv.shape)],
        out_specs=pl.BlockSpec((m_len, w_kv.shape[1]), lambda i: (i, 0)),
        out_shape=jax.ShapeDtypeStruct((n, w_kv.shape[1]), BF16),
        compiler_params=_params(),
        name="mem_kv",
    )(mem2d, g, w_kv)


def _mixer_kernel(layer, tiles_per_seq, ts, rc, d_pool, d_sc,
                  xa_ref, xc_ref, g_ref, win_hbm, wbd_ref, pscale_ref, sconv_ref, confw_ref, confb_ref,
                  lng_ref, lnb_ref, gng_ref, wout_hbm, o_ref,
                  win_ref, wout_ref, stage_in, stage_out, sem, z0_ref, z1_ref, ext_ref, y0_ref, y1_ref,
                  pb_ref):
    c0, c1, c2, c3, c4 = d_pool, d_pool + d_sc, d_pool + 2 * d_sc, d_pool + 3 * d_sc, d_pool + 4 * d_sc
    c5 = d_pool + 5 * d_sc
    d_mix = d_pool + 2 * d_sc
    n_taps = confw_ref.shape[0]
    grp = d_pool // len(POOL_WINDOWS)
    j = pl.program_id(0)

    @pl.when(j == 0)
    def _():
        _load_cast(win_hbm, layer, win_ref, stage_in, sem)
        _load_cast(wout_hbm, layer, wout_ref, stage_out, sem)
        z1_ref[...] = jnp.zeros(z1_ref.shape, F32)
        y0_ref[...] = jnp.zeros(y0_ref.shape, BF16)
        ext_ref[...] = jnp.zeros(ext_ref.shape, F32)

    lane = lax.broadcasted_iota(jnp.int32, (rc, d_pool), 1)
    row = lax.broadcasted_iota(jnp.int32, (rc, d_pool), 0)
    win = jnp.where(lane < grp, POOL_WINDOWS[0],
                    jnp.where(lane < 2 * grp, POOL_WINDOWS[1],
                              jnp.where(lane < 3 * grp, POOL_WINDOWS[2], POOL_WINDOWS[3])))

    def stage_a(rows, z_ref):
        h = _rms(xa_ref[rows, :], g_ref[...]).astype(BF16)
        z_ref[...] = _dot(h, win_ref[...])

    def stage_c(rows, y_ref):
        o_ref[rows, :] = xc_ref[rows, :] + _dot(y_ref[...], wout_ref[...])

    def chunk(c, tile_in_seq, z_ref, y_ref):
        r0 = c * rc

        def zc(lo, hi):
            return z_ref[pl.ds(r0, rc), lo:hi]

        ext_ref[pl.ds(HALO + r0, rc), 0:c0] = zc(0, c0)
        ext_ref[pl.ds(HALO + r0, rc), c0:c1] = zc(c2, c3) * zc(c0, c1)
        ext_ref[pl.ds(HALO + r0, rc), c1:d_mix] = zc(c3, c4) * _sigmoid(zc(c4, c5))

        e = ext_ref[pl.ds(r0 + 8, rc + 24), 0:c0]
        s2 = e + pltpu.roll(e, 1, 0)
        s4 = s2 + pltpu.roll(s2, 2, 0)
        s8 = s4 + pltpu.roll(s4, 4, 0)
        s16 = s8[24:] + s8[16:16 + rc]
        ssum = jnp.where(lane < grp, s2[24:],
                         jnp.where(lane < 2 * grp, s4[24:],
                                   jnp.where(lane < 3 * grp, s8[24:], s16)))
        tpos = tile_in_seq * ts + r0 + row
        cnt = jnp.minimum(tpos + 1, win).astype(F32)
        pooled = ssum / cnt - e[24:]
        ya = _dot(pooled.astype(BF16), wbd_ref[...]) * pscale_ref[...]
        y_ref[pl.ds(r0, rc), 0:c0] = _rms(ya, gng_ref[:, 0:c0]).astype(BF16)

        sx = ext_ref[pl.ds(r0 + 24, rc + 8), c0:c1]
        conv = (sx[8:] * sconv_ref[2:3, :]
                + pltpu.roll(sx, 1, 0)[8:] * sconv_ref[1:2, :]
                + pltpu.roll(sx, 2, 0)[8:] * sconv_ref[0:1, :])
        yb = zc(c1, c2) * conv
        y_ref[pl.ds(r0, rc), c0:c1] = _rms(yb, gng_ref[:, c0:c1]).astype(BF16)

        hcs = []
        for wl in range(0, d_sc, LANES):
            wh = wl + LANES
            acc = None
            for b in range(SUBLANES):
                pb = None
                for a in range((n_taps + SUBLANES - 1) // SUBLANES):
                    d = SUBLANES * a + b
                    if d >= n_taps:
                        continue
                    xa = ext_ref[pl.ds(r0 + 24 - SUBLANES * a, rc + 8), c1 + wl:c1 + wh]
                    k = n_taps - 1 - d
                    term = xa * confw_ref[k:k + 1, wl:wh]
                    pb = term if pb is None else pb + term
                if b:
                    pb_ref[b, :, :] = pb
                    shifted = pb_ref[b, 8 - b:8 - b + rc, :]
                else:
                    shifted = pb[8:]
                acc = shifted if acc is None else acc + shifted
            hcs.append(acc + confb_ref[:, wl:wh])
        hc = jnp.concatenate(hcs, axis=-1)
        mu = jnp.mean(hc, axis=-1, keepdims=True)
        xc = hc - mu
        var = jnp.mean(xc * xc, axis=-1, keepdims=True)
        ln = xc * lax.rsqrt(var + EPS) * lng_ref[...] + lnb_ref[...]
        yc = ln * _sigmoid(ln)
        y_ref[pl.ds(r0, rc), c1:d_mix] = _rms(yc, gng_ref[:, c1:d_mix]).astype(BF16)

    def stage_b(tile, z_ref, y_ref):
        tile_in_seq = (tile + tiles_per_seq) % tiles_per_seq
        ext_ref[0:HALO, :] = jnp.where(tile_in_seq == 0, 0.0, ext_ref[0:HALO, :])
        for c in range(ts // rc):
            chunk(c, tile_in_seq, z_ref, y_ref)
        ext_ref[0:HALO, :] = ext_ref[ts:ts + HALO, :]

    lo_rows, hi_rows = slice(0, ts), slice(ts, 2 * ts)
    stage_c(lo_rows, y0_ref)
    stage_a(lo_rows, z0_ref)
    stage_b(2 * j - 1, z1_ref, y1_ref)
    stage_a(hi_rows, z1_ref)
    stage_b(2 * j, z0_ref, y0_ref)
    stage_c(hi_rows, y1_ref)


def _mixer(x, layer, seq_len, g, w_in, w_bd, pscale, sconv_w, conf_w, conf_b, ln_g, ln_b, gn_g, w_out):
    n, d = x.shape
    ts = min(TS_MIX, seq_len)
    rc = min(RC_MIX, ts)
    n_pairs = n // (2 * ts)
    d_pool = w_bd.shape[0]
    d_sc = sconv_w.shape[1]
    d_in = w_in.shape[2]
    d_mix = w_out.shape[1]
    assert conf_w.shape[0] <= HALO - 1 and seq_len % ts == 0 and ts % rc == 0 and n % (2 * ts) == 0
    kern = functools.partial(_mixer_kernel, layer, seq_len // ts, ts, rc, d_pool, d_sc)
    small = (w_bd, pscale, sconv_w, conf_w, conf_b, ln_g, ln_b, gn_g)
    last = n_pairs - 1
    return pl.pallas_call(
        kern,
        grid=(n_pairs + 1,),
        in_specs=([pl.BlockSpec((2 * ts, d), lambda j: (jnp.minimum(j, last), 0)),
                   pl.BlockSpec((2 * ts, d), lambda j: (jnp.maximum(j - 1, 0), 0)),
                   _const_spec(g.shape), _ANY]
                  + [_const_spec(c.shape) for c in small] + [_ANY]),
        out_specs=pl.BlockSpec((2 * ts, d), lambda j: (jnp.maximum(j - 1, 0), 0)),
        out_shape=jax.ShapeDtypeStruct((n, d), F32),
        scratch_shapes=[pltpu.VMEM((d, d_in), BF16), pltpu.VMEM((d_mix, d), BF16),
                        pltpu.VMEM(_stage_shape(w_in), F32), pltpu.VMEM(_stage_shape(w_out), F32),
                        pltpu.SemaphoreType.DMA,
                        pltpu.VMEM((ts, d_in), F32), pltpu.VMEM((ts, d_in), F32),
                        pltpu.VMEM((HALO + ts, d_mix), F32),
                        pltpu.VMEM((ts, d_mix), BF16), pltpu.VMEM((ts, d_mix), BF16),
                        pltpu.VMEM((SUBLANES, rc + SUBLANES, LANES), F32)],
        compiler_params=_params(),
        name="mixer",
    )(x, x, g, w_in, *small, w_out)


def _attn_body(layer, x_ref, g_ref, wq_hbm, k_ref, v_ref, wo_hbm, wq_ref, wo_ref, stage_ref, sem):
    @pl.when(pl.program_id(0) == 0)
    def _():
        _load_cast(wq_hbm, layer, wq_ref, stage_ref, sem)
        _load_cast(wo_hbm, layer, wo_ref, stage_ref, sem)

    d = x_ref.shape[1]
    dh = d // N_MEM_HEADS
    x = x_ref[...]
    h = _rms(x, g_ref[...]).astype(BF16)
    q = _dot(h, wq_ref[...])
    outs = []
    for hh in range(N_MEM_HEADS):
        sl = slice(hh * dh, (hh + 1) * dh)
        s = lax.dot_general(q[:, sl].astype(BF16), k_ref[:, sl], (((1,), (1,)), ((), ())),
                            preferred_element_type=F32) * (dh ** -0.5)
        e = jnp.exp(s - jnp.max(s, axis=-1, keepdims=True))
        p = e * (1.0 / jnp.sum(e, axis=-1, keepdims=True))
        outs.append(_dot(p.astype(BF16), v_ref[:, sl]).astype(BF16))
    return x + _dot(jnp.concatenate(outs, axis=-1), wo_ref[...])


def _attn_kernel(layer, x_ref, g_ref, wq_hbm, k_ref, v_ref, wo_hbm, o_ref, wq_ref, wo_ref, stage_ref, sem):
    o_ref[...] = _attn_body(layer, x_ref, g_ref, wq_hbm, k_ref, v_ref, wo_hbm, wq_ref, wo_ref, stage_ref, sem)


def _attn_route_kernel(layer, x_ref, g_ref, wq_hbm, k_ref, v_ref, wo_hbm, fg_ref, wr_ref, tri_ref,
                       o_ref, h2_ref, ri_ref, rg_ref, cnt_ref, wq_ref, wo_ref, stage_ref, sem, carry_ref):
    i = pl.program_id(0)

    @pl.when(i == 0)
    def _():
        carry_ref[...] = jnp.zeros(carry_ref.shape, F32)

    xn = _attn_body(layer, x_ref, g_ref, wq_hbm, k_ref, v_ref, wo_hbm, wq_ref, wo_ref, stage_ref, sem)
    o_ref[...] = xn
    h2 = _rms(xn, fg_ref[...])
    h2_ref[...] = h2

    h_hi = h2.astype(BF16)
    h_lo = (h2 - h_hi.astype(F32)).astype(BF16)
    part = _dot(h_hi, wr_ref[...])
    logits = part[:, 0:LANES] + part[:, LANES:2 * LANES] + _dot(h_lo, wr_ref[:, 0:LANES])

    ts = x_ref.shape[0]
    lane = lax.broadcasted_iota(jnp.int32, (ts, LANES), 1).astype(F32)
    neg = jnp.float32(-jnp.inf)
    lg = jnp.where(lane < N_EXPERTS, logits, neg)
    v1 = jnp.max(lg, axis=-1, keepdims=True)
    i1 = jnp.min(jnp.where(lg == v1, lane, float(LANES)), axis=-1, keepdims=True)
    lg2 = jnp.where(lane == i1, neg, lg)
    v2 = jnp.max(lg2, axis=-1, keepdims=True)
    i2 = jnp.min(jnp.where(lg2 == v2, lane, float(LANES)), axis=-1, keepdims=True)
    e2 = jnp.exp(v2 - v1)
    den = 1.0 + e2
    w1 = 1.0 / den
    w2 = e2 / den

    sel1 = lane == i1
    sel2 = lane == i2
    onehot = jnp.where(sel1 | sel2, 1.0, 0.0)
    ranks = carry_ref[0:1, :] + _dot(tri_ref[...], onehot.astype(BF16))
    r1 = jnp.sum(jnp.where(sel1, ranks, 0.0), axis=-1, keepdims=True)
    r2 = jnp.sum(jnp.where(sel2, ranks, 0.0), axis=-1, keepdims=True)
    total = carry_ref[0:1, :] + jnp.sum(onehot, axis=0, keepdims=True)
    carry_ref[...] = jnp.broadcast_to(total, carry_ref.shape)
    cnt_ref[...] = jnp.broadcast_to(total, cnt_ref.shape)

    meta = jnp.where(lane == 0, i1, jnp.where(lane == 1, i2, jnp.where(lane == 2, r1, jnp.where(lane == 3, r2, 0.0))))
    ri_ref[...] = meta.astype(jnp.int32)
    rg_ref[...] = jnp.where(lane == 0, w1, jnp.where(lane == 1, w2, 0.0))


def _attn(x, layer, seq_len, m_len, g, w_q, kv, w_o, route=None):
    n, d = x.shape
    ts = min(TS_ATT, seq_len)
    tps = seq_len // ts
    x_spec = pl.BlockSpec((ts, d), lambda i: (i, 0))
    in_specs = [x_spec, _const_spec((1, d)), _ANY,
                pl.BlockSpec((m_len, d), lambda i: (i // tps, 0)),
                pl.BlockSpec((m_len, d), lambda i: (i // tps, 1)),
                _ANY]
    args = [x, g, w_q, kv, kv, w_o]
    w_scratch = [pltpu.VMEM((d, d), BF16), pltpu.VMEM((d, d), BF16),
                 pltpu.VMEM(_stage_shape(w_q), F32), pltpu.SemaphoreType.DMA]
    if route is None:
        return pl.pallas_call(
            functools.partial(_attn_kernel, layer), grid=(n // ts,), in_specs=in_specs, out_specs=x_spec,
            out_shape=jax.ShapeDtypeStruct((n, d), F32), scratch_shapes=w_scratch,
            compiler_params=_params(), name="attn",
        )(*args)
    ffn_g, w_r2 = route
    tri = jnp.tril(jnp.ones((ts, ts), BF16), -1)
    lane_spec = pl.BlockSpec((ts, LANES), lambda i: (i, 0))
    return pl.pallas_call(
        functools.partial(_attn_route_kernel, layer), grid=(n // ts,),
        in_specs=in_specs + [_const_spec((1, d)), _const_spec(w_r2.shape), _const_spec(tri.shape)],
        out_specs=[x_spec, x_spec, lane_spec, lane_spec,
                   pl.BlockSpec((SUBLANES, LANES), lambda i: (0, 0))],
        out_shape=[jax.ShapeDtypeStruct((n, d), F32), jax.ShapeDtypeStruct((n, d), F32),
                   jax.ShapeDtypeStruct((n, LANES), jnp.int32), jax.ShapeDtypeStruct((n, LANES), F32),
                   jax.ShapeDtypeStruct((SUBLANES, LANES), F32)],
        scratch_shapes=w_scratch + [pltpu.VMEM((SUBLANES, LANES), F32)],
        compiler_params=_params(), name="attn_route",
    )(*args, ffn_g, w_r2, tri)


def _ffn_kernel(layer, chunks, d_ff, x_ref, g_ref, wgu_hbm, wd_hbm, o_ref,
                wgu_ref, wd_ref, stage_gu, stage_d, sem):
    @pl.when(pl.program_id(0) == 0)
    def _():
        _load_cast(wgu_hbm, layer, wgu_ref, stage_gu, sem)
        _load_cast(wd_hbm, layer, wd_ref, stage_d, sem)

    x = x_ref[...]
    h = _rms(x, g_ref[...]).astype(BF16)
    acc = x
    for lo, hi in chunks:
        gate = _dot(h, wgu_ref[:, lo:hi])
        up = _dot(h, wgu_ref[:, d_ff + lo:d_ff + hi])
        a = (gate * _sigmoid(gate) * up).astype(BF16)
        acc = acc + _dot(a, wd_ref[lo:hi, :])
    o_ref[...] = acc


def _ffn(x, layer, g, w_gu, w_down):
    n, d = x.shape
    ts = min(TS_FFN, n)
    d_ff = w_down.shape[1]
    step = 1024 if d_ff > 1024 else d_ff
    chunks = tuple((lo, min(lo + step, d_ff)) for lo in range(0, d_ff, step))
    x_spec = pl.BlockSpec((ts, d), lambda i: (i, 0))
    return pl.pallas_call(
        functools.partial(_ffn_kernel, layer, chunks, d_ff), grid=(n // ts,),
        in_specs=[x_spec, _const_spec((1, d)), _ANY, _ANY],
        out_specs=x_spec, out_shape=jax.ShapeDtypeStruct((n, d), F32),
        scratch_shapes=[pltpu.VMEM(w_gu.shape[1:], BF16), pltpu.VMEM(w_down.shape[1:], BF16),
                        pltpu.VMEM(_stage_shape(w_gu), F32), pltpu.VMEM(_stage_shape(w_down), F32),
                        pltpu.SemaphoreType.DMA],
        compiler_params=_params(), name="ffn_dense",
    )(x, g, w_gu, w_down)


def _row_copy(src, src_row, dst, dst_row, sem):
    return pltpu.make_async_copy(src.at[pl.ds(src_row, 1)], dst.at[pl.ds(dst_row, 1)], sem)


def _dispatch_kernel(td, tm, starts_ref, ntiles_ref, pos_ref, h_ref, xs_hbm, zero_ref, zsem, sem):
    i = pl.program_id(0)
    zr = zero_ref.shape[0]

    @pl.when(i == 0)
    def _():
        zero_ref[...] = jnp.zeros(zero_ref.shape, F32)

        def zero_copies(e):
            base = pl.multiple_of(starts_ref[e] + (ntiles_ref[e] - 1) * tm, tm)
            return [pltpu.make_async_copy(zero_ref, xs_hbm.at[pl.ds(base + q * zr, zr)], zsem)
                    for q in range(tm // zr)]

        for e in range(N_EXPERTS):
            @pl.when(ntiles_ref[e] > 0)
            def _():
                for cp in zero_copies(e):
                    cp.start()
        for e in range(N_EXPERTS):
            @pl.when(ntiles_ref[e] > 0)
            def _():
                for cp in zero_copies(e):
                    cp.wait()

        n_used = ntiles_ref[0]
        for e in range(1, N_EXPERTS):
            n_used = n_used + ntiles_ref[e]

        def clear_tile(j, carry):
            base = pl.multiple_of(j * tm, tm)
            copies = [pltpu.make_async_copy(zero_ref, xs_hbm.at[pl.ds(base + q * zr, zr)], zsem)
                      for q in range(tm // zr)]
            for cp in copies:
                cp.start()
            for cp in copies:
                cp.wait()
            return carry

        lax.fori_loop(n_used, xs_hbm.shape[0] // tm, clear_tile, 0)

    def issue(t, carry):
        for k in range(2):
            _row_copy(h_ref, t, xs_hbm, pos_ref[0, 0, 2 * t + k], sem).start(priority=k)
        return carry

    def drain(t, carry):
        for k in range(2):
            _row_copy(h_ref, 0, xs_hbm, 0, sem).wait()
        return carry

    lax.fori_loop(0, td, issue, 0, unroll=DMA_UNROLL)
    lax.fori_loop(0, td, drain, 0, unroll=DMA_UNROLL)


def _dispatch(h2, pos, starts, ntiles, r_pad, td, tm):
    n, d = h2.shape
    zr = min(ZR_MOE, tm)
    grid_spec = pltpu.PrefetchScalarGridSpec(
        num_scalar_prefetch=2, grid=(n // td,),
        in_specs=[pl.BlockSpec((1, 1, 2 * td), lambda i, *_: (i, 0, 0), memory_space=pltpu.SMEM),
                  pl.BlockSpec((td, d), lambda i, *_: (i, 0))],
        out_specs=_ANY,
        scratch_shapes=[pltpu.VMEM((zr, d), F32), pltpu.SemaphoreType.DMA, pltpu.SemaphoreType.DMA])
    return pl.pallas_call(
        functools.partial(_dispatch_kernel, td, tm), grid_spec=grid_spec,
        out_shape=jax.ShapeDtypeStruct((r_pad, d), F32),
        compiler_params=_params(), name="moe_dispatch",
    )(starts, ntiles, pos, h2)


def _gmm_kernel(te_ref, nu_ref, valid_ref, xs_ref, wg_ref, wu_ref, wd_ref, o_ref, xb_ref):
    i = pl.program_id(0)
    c = pl.program_id(1)
    tm = xs_ref.shape[0]
    quarter = tm // GMM_ROW_PARTS

    @pl.when(i < nu_ref[0])
    def _():
        @pl.when(c == 0)
        def _():
            xb_ref[...] = xs_ref[...].astype(BF16)
            o_ref[...] = jnp.zeros(o_ref.shape, F32)

        nv = valid_ref[i]
        for part in range(1, GMM_ROW_PARTS + 1):
            rows = part * quarter

            @pl.when(jnp.logical_and(nv > rows - quarter, nv <= rows))
            def _():
                x = xb_ref[0:rows, :]
                gate = _dot(x, wg_ref[...].astype(BF16))
                up = _dot(x, wu_ref[...].astype(BF16))
                a = (gate * _sigmoid(gate) * up).astype(BF16)
                o_ref[0:rows, :] += _dot(a, wd_ref[...].astype(BF16))

    @pl.when(jnp.logical_and(i >= nu_ref[0], c == 0))
    def _():
        o_ref[...] = jnp.zeros(o_ref.shape, F32)


def _gmm(xs, layer, w_gu, w_down, tile_expert, n_used, tile_valid, tm, fc):
    r_pad, d = xs.shape
    d_ff = w_down.shape[2]
    n_chunks = d_ff // fc
    n_tiles = r_pad // tm
    assert tm % (GMM_ROW_PARTS * 2 * SUBLANES) == 0

    def tile(i, nu):
        return jnp.minimum(i, nu[0] - 1)

    def chunk(i, c, nu):
        return jnp.where(i < nu[0], c, n_chunks - 1)

    grid_spec = pltpu.PrefetchScalarGridSpec(
        num_scalar_prefetch=3, grid=(n_tiles, n_chunks),
        in_specs=[pl.BlockSpec((tm, d), lambda i, c, te, nu, nv: (tile(i, nu), 0)),
                  pl.BlockSpec((None, None, d, fc),
                               lambda i, c, te, nu, nv: (layer, te[tile(i, nu)], 0, chunk(i, c, nu))),
                  pl.BlockSpec((None, None, d, fc),
                               lambda i, c, te, nu, nv: (layer, te[tile(i, nu)], 0, n_chunks + chunk(i, c, nu))),
                  pl.BlockSpec((None, None, fc, d),
                               lambda i, c, te, nu, nv: (layer, te[tile(i, nu)], chunk(i, c, nu), 0))],
        out_specs=pl.BlockSpec((tm, d), lambda i, c, te, nu, nv: (i, 0)),
        scratch_shapes=[pltpu.VMEM((tm, d), BF16)])
    return pl.pallas_call(
        _gmm_kernel, grid_spec=grid_spec, out_shape=jax.ShapeDtypeStruct((r_pad, d), F32),
        compiler_params=_params(2), name="moe_gmm",
    )(tile_expert, n_used, tile_valid, xs, w_gu, w_gu, w_down)


def _combine_kernel(td, final, n_steps, pos_ref, pos_next_ref, x_ref, rg_ref, fg_ref, ys_hbm, o_ref, ybuf, sems):
    i = pl.program_id(0)
    slot = i % 2

    def issue_tile(p_ref, s):
        def issue(t, carry):
            for k in range(2):
                _row_copy(ys_hbm, p_ref[0, 0, 2 * t + k], ybuf.at[s, k], t, sems.at[s]).start(priority=k)
            return carry

        lax.fori_loop(0, td, issue, 0, unroll=DMA_UNROLL)

    @pl.when(i == 0)
    def _():
        issue_tile(pos_ref, 0)

    @pl.when(i + 1 < n_steps)
    def _():
        issue_tile(pos_next_ref, 1 - slot)

    def drain(t, carry):
        for k in range(2):
            _row_copy(ys_hbm, 0, ybuf.at[slot, k], 0, sems.at[slot]).wait()
        return carry

    lax.fori_loop(0, td, drain, 0, unroll=DMA_UNROLL)
    out = x_ref[...] + rg_ref[:, 0:1] * ybuf[slot, 0] + rg_ref[:, 1:2] * ybuf[slot, 1]
    if final:
        out = _rms(out, fg_ref[...])
    o_ref[...] = out


def _combine(x, rg, pos, ys, final_g, final, td):
    n, d = x.shape
    n_steps = n // td
    x_spec = pl.BlockSpec((td, d), lambda i: (i, 0))
    return pl.pallas_call(
        functools.partial(_combine_kernel, td, final, n_steps), grid=(n_steps,),
        in_specs=[pl.BlockSpec((1, 1, 2 * td), lambda i: (i, 0, 0), memory_space=pltpu.SMEM),
                  pl.BlockSpec((1, 1, 2 * td), lambda i: (jnp.minimum(i + 1, n_steps - 1), 0, 0),
                               memory_space=pltpu.SMEM),
                  x_spec,
                  pl.BlockSpec((td, LANES), lambda i: (i, 0)),
                  pl.BlockSpec((1, d), lambda i: (0, 0)),
                  _ANY],
        out_specs=x_spec,
        out_shape=jax.ShapeDtypeStruct((n, d), F32),
        scratch_shapes=[pltpu.VMEM((2, 2, td, d), F32), pltpu.SemaphoreType.DMA((2,))],
        compiler_params=_params(), name="moe_combine",
    )(pos, pos, x, rg, final_g, ys)


def _moe(x, layer, h2, ri, rg, counts, w_gu, w_down, final_g, final):
    n, d = x.shape
    tm = min(TM_MOE, n)
    td = min(TD_MOE, n)
    fc = min(FC_MOE, w_down.shape[2])
    max_tiles = (2 * n) // tm + N_EXPERTS
    cnt = counts[0, :N_EXPERTS].astype(jnp.int32)
    ntiles = (cnt + tm - 1) // tm
    cum = jnp.cumsum(ntiles)
    starts = (cum - ntiles) * tm
    n_used = cum[-1:]
    tile_ids = jnp.arange(max_tiles, dtype=jnp.int32)
    tile_expert = jnp.minimum(jnp.sum((tile_ids[:, None] >= cum[None, :]).astype(jnp.int32), axis=1),
                              N_EXPERTS - 1)
    experts = jnp.arange(N_EXPERTS, dtype=jnp.int32)
    of_tile = tile_expert[:, None] == experts
    rows_before = tile_ids * tm - jnp.sum(jnp.where(of_tile, starts, 0), axis=-1)
    tile_valid = jnp.clip(jnp.sum(jnp.where(of_tile, cnt, 0), axis=-1) - rows_before, 0, tm)
    start_of = jnp.sum(jnp.where(ri[:, 0:2, None] == experts, starts, 0), axis=-1)
    pos = (start_of + ri[:, 2:4]).reshape(n // td, 1, 2 * td)
    xs = _dispatch(h2, pos, starts, ntiles, max_tiles * tm, td, tm)
    ys = _gmm(xs, layer, w_gu, w_down, tile_expert, n_used, tile_valid, tm, fc)
    return _combine(x, rg, pos, ys, final_g, final, td)


def _row(v):
    return v.reshape(1, -1).astype(F32)


def _block_diag(pool_w):
    g, c, _ = pool_w.shape
    out = jnp.zeros((g * c, g * c), pool_w.dtype)
    for j in range(g):
        out = lax.dynamic_update_slice(out, pool_w[j], (j * c, j * c))
    return out


def _router_hi_lo(w_router):
    d, e = w_router.shape
    hi = w_router.astype(BF16)
    lo = (w_router - hi.astype(F32)).astype(BF16)
    pad = jnp.zeros((d, LANES - e), BF16)
    return jnp.concatenate([hi, pad, lo, pad], axis=1)


def kernel(x, mem, mix_norm_g, w_in, pool_w, pool_scale, sconv_w, conf_w, conf_b, conf_ln_g, conf_ln_b, group_norm_g, w_out, mem_norm_g, w_mem_kv, xattn_norm_g, w_q, w_o, ffn_norm_g, w_gu_dense, w_down_dense, w_router, w_gu_moe, w_down_moe, final_norm_g):
    b, s, d = x.shape
    m_len = mem.shape[1]
    depth = w_in.shape[0]
    assert depth % 2 == 0, "the last layer must be a routed layer (it applies the final norm)"
    kv = _memkv(mem.reshape(b * m_len, d), _row(mem_norm_g), w_mem_kv, m_len)
    xf = x.reshape(b * s, d)
    for l in range(depth):
        xf = _mixer(xf, l, s, _row(mix_norm_g[l]), w_in, _block_diag(pool_w[l]).astype(BF16),
                    _row(pool_scale[l]), sconv_w[l], conf_w[l], _row(conf_b[l]), _row(conf_ln_g[l]),
                    _row(conf_ln_b[l]), _row(group_norm_g[l]), w_out)
        if l % 2 == 0:
            xf = _attn(xf, l, s, m_len, _row(xattn_norm_g[l]), w_q, kv, w_o)
            xf = _ffn(xf, l // 2, _row(ffn_norm_g[l]), w_gu_dense, w_down_dense)
        else:
            xf, h2, ri, rg, counts = _attn(
                xf, l, s, m_len, _row(xattn_norm_g[l]), w_q, kv, w_o,
                route=(_row(ffn_norm_g[l]), _router_hi_lo(w_router[l // 2])))
            xf = _moe(xf, l // 2, h2, ri, rg, counts, w_gu_moe, w_down_moe, _row(final_norm_g), l == depth - 1)
    return xf.reshape(b, s, d)
```
